```python
import math
import jax, jax.numpy as jnp
from jax import lax
import numpy as np

D_MODEL = 1024
BATCH = 8
SEQ = 4096
DEPTH = 2
DEC_BATCH = 8
DEC_SEQ = 2048
PAST_LEN = 128

HEAD_DIM = 64
N_HEADS_A = 8
N_KV_A = 2
N_HEADS_B = 8
D_A = N_HEADS_A * HEAD_DIM
D_KV_A = N_KV_A * HEAD_DIM
D_B = N_HEADS_B * HEAD_DIM
D_MIX = D_A + D_B
D_IN = D_A + 2 * D_KV_A + 3 * D_B
WINDOW_A = 128
DILATION_PAIRS = ((128, 1), (512, 4), (2048, 16))
MEM_LEN = 256
N_HEADS_X = 4
HEAD_DIM_X = 128
D_X = N_HEADS_X * HEAD_DIM_X
D_FF = 2816
CONV_WIDTH = 3
ROPE_THETA = 10000.0
EPS = 1e-6
NEG = -1e30

kernel_name = 'hybrid_parallel_swa_dilated_encoder'


def rms_norm(x, g):
    xf = x.astype(jnp.float32)
    y = xf * lax.rsqrt(jnp.mean(xf * xf, axis=-1, keepdims=True) + EPS)
    return (y * g.astype(jnp.float32)).astype(x.dtype)


def rope_tables(seq_len):
    inv = 1.0 / (ROPE_THETA ** (jnp.arange(0, HEAD_DIM, 2, dtype=jnp.float32) / HEAD_DIM))
    ang = jnp.arange(seq_len, dtype=jnp.float32)[:, None] * inv[None, :]
    return jnp.cos(ang), jnp.sin(ang)


def apply_rope(x, cos, sin):
    xf = x.astype(jnp.float32)
    x1, x2 = jnp.split(xf, 2, axis=-1)
    c = cos[None, :, None, :]
    s = sin[None, :, None, :]
    return jnp.concatenate([x1 * c - x2 * s, x2 * c + x1 * s], axis=-1).astype(x.dtype)


def banded_attention(q, k, v, half_window, valid_len, sink):
    n, L, hq, dh = q.shape
    hk = k.shape[2]
    grp = hq // hk
    blk = half_window
    nblk = L // blk
    scale = dh ** -0.5
    qb = q.reshape(n, nblk, blk, hk, grp, dh)
    pad = ((0, 0), (blk, blk), (0, 0), (0, 0))
    kp = jnp.pad(k, pad)
    vp = jnp.pad(v, pad)
    offs_q = jnp.arange(blk)
    offs_k = jnp.arange(3 * blk) - blk

    def one_block(i):
        qi = lax.dynamic_index_in_dim(qb, i, axis=1, keepdims=False)
        ki = lax.dynamic_slice_in_dim(kp, i * blk, 3 * blk, axis=1)
        vi = lax.dynamic_slice_in_dim(vp, i * blk, 3 * blk, axis=1)
        s = jnp.einsum('nqkgd,nskd->nkgqs', qi, ki).astype(jnp.float32) * scale
        qpos = i * blk + offs_q
        kpos = i * blk + offs_k
        mask = (jnp.abs(qpos[:, None] - kpos[None, :]) <= half_window) & ((kpos >= 0) & (kpos < valid_len))[None, :]
        s = jnp.where(mask, s, NEG)
        m = jnp.max(s, axis=-1)
        if sink is not None:
            sk = sink.astype(jnp.float32).reshape(1, hk, grp, 1)
            m = jnp.maximum(m, sk)
        p = jnp.exp(s - m[..., None])
        den = jnp.sum(p, axis=-1)
        if sink is not None:
            den = den + jnp.exp(sk - m)
        o = jnp.einsum('nkgqs,nskd->nqkgd', p, vi.astype(jnp.float32))
        o = o / jnp.transpose(den, (0, 3, 1, 2))[..., None]
        lse = jnp.transpose(m + jnp.log(den), (0, 3, 1, 2))
        return o.reshape(n, blk, hq, dh), lse.reshape(n, blk, hq)

    o, lse = lax.map(one_block, jnp.arange(nblk))
    o = jnp.moveaxis(o, 0, 1).reshape(n, L, hq, dh)
    lse = jnp.moveaxis(lse, 0, 1).reshape(n, L, hq)
    return o, lse


def dilated_branch(q, k, v, window, dilation):
    n, S, H, dh = q.shape
    r = dilation
    L = S // r
    hw = window // (2 * r)
    Lp = -(-L // hw) * hw

    def to_sub(t):
        t = t.reshape(n, L, r, H, dh).transpose(0, 2, 1, 3, 4).reshape(n * r, L, H, dh)
        return jnp.pad(t, ((0, 0), (0, Lp - L), (0, 0), (0, 0)))

    o, lse = banded_attention(to_sub(q), to_sub(k), to_sub(v), hw, L, None)
    o = o[:, :L].reshape(n, r, L, H, dh).transpose(0, 2, 1, 3, 4).reshape(n, S, H, dh)
    lse = lse[:, :L].reshape(n, r, L, H).transpose(0, 2, 1, 3).reshape(n, S, H)
    return o, lse


def dilated_mixture(q, k, v):
    outs = []
    lses = []
    for window, r in DILATION_PAIRS:
        o, l = dilated_branch(q, k, v, window, r)
        outs.append(o)
        lses.append(l)
    o = jnp.stack(outs)
    w = jax.nn.softmax(jnp.stack(lses), axis=0)
    return jnp.sum(w[..., None] * o, axis=0)


def token_mixer(h, w_in, sink, g_oa, g_ob, w_out, cos, sin):
    n, S, _ = h.shape
    cuts = [D_A, D_A + D_KV_A, D_A + 2 * D_KV_A, D_A + 2 * D_KV_A + D_B, D_A + 2 * D_KV_A + 2 * D_B]
    qa, ka, va, qb, kb, vb = jnp.split(h @ w_in, cuts, axis=-1)
    qa = apply_rope(qa.reshape(n, S, N_HEADS_A, HEAD_DIM), cos, sin)
    ka = apply_rope(ka.reshape(n, S, N_KV_A, HEAD_DIM), cos, sin)
    va = va.reshape(n, S, N_KV_A, HEAD_DIM)
    oa, _ = banded_attention(qa, ka, va, WINDOW_A, S, sink)
    oa = rms_norm(oa.reshape(n, S, D_A).astype(h.dtype), g_oa)
    qb = apply_rope(qb.reshape(n, S, N_HEADS_B, HEAD_DIM), cos, sin)
    kb = apply_rope(kb.reshape(n, S, N_HEADS_B, HEAD_DIM), cos, sin)
    vb = vb.reshape(n, S, N_HEADS_B, HEAD_DIM)
    ob = dilated_mixture(qb, kb, vb)
    ob = rms_norm(ob.reshape(n, S, D_B).astype(h.dtype), g_ob)
    return jnp.concatenate([oa, ob], axis=-1) @ w_out


def cross_attention(h, mem, g_mem, w_cq, w_ckv, w_co):
    n, S, _ = h.shape
    M = mem.shape[1]
    q = (h @ w_cq).reshape(n, S, N_HEADS_X, HEAD_DIM_X)
    k, v = jnp.split(rms_norm(mem, g_mem) @ w_ckv, 2, axis=-1)
    k = k.reshape(n, M, N_HEADS_X, HEAD_DIM_X)
    v = v.reshape(n, M, N_HEADS_X, HEAD_DIM_X)
    s = jnp.einsum('nqhd,nmhd->nhqm', q, k).astype(jnp.float32) * (HEAD_DIM_X ** -0.5)
    p = jax.nn.softmax(s, axis=-1)
    o = jnp.einsum('nhqm,nmhd->nqhd', p, v.astype(jnp.float32)).reshape(n, S, D_X).astype(h.dtype)
    return o @ w_co


def conv_ffn(h, w_gu, conv_w, conv_b, w_down):
    g, u = jnp.split(h @ w_gu, 2, axis=-1)
    gp = jnp.pad(g, ((0, 0), (1, 1), (0, 0)))
    gc = gp[:, :-2] * conv_w[0] + gp[:, 1:-1] * conv_w[1] + gp[:, 2:] * conv_w[2] + conv_b
    a = jax.nn.gelu(gc.astype(jnp.float32), approximate=False) * u.astype(jnp.float32)
    return a.astype(h.dtype) @ w_down


def trunk(x, mem, g_mix, w_in, sink_a, g_out_a, g_out_b, w_out, g_cross, g_mem, w_cq, w_ckv, w_co,
          g_ffn, w_gu, conv_w, conv_b, w_down, g_final):
    cos, sin = rope_tables(x.shape[1])
    for l in range(DEPTH):
        x = x + token_mixer(rms_norm(x, g_mix[l]), w_in[l], sink_a[l], g_out_a[l], g_out_b[l], w_out[l], cos, sin)
        x = x + cross_attention(rms_norm(x, g_cross[l]), mem, g_mem[l], w_cq[l], w_ckv[l], w_co[l])
        x = x + conv_ffn(rms_norm(x, g_ffn[l]), w_gu[l], conv_w[l], conv_b[l], w_down[l])
    return rms_norm(x, g_final)


def setup_inputs(seed: int = 0) -> dict:
    key = jax.random.key(seed)
    ks = jax.random.split(key, 24)

    def nrm(k, shape, s):
        return jax.random.normal(k, shape, jnp.float32) * s

    def gain(k, shape):
        return 1.0 + 0.05 * jax.random.normal(k, shape, jnp.float32)

    return {
        'x_prompt': nrm(ks[0], (BATCH, SEQ, D_MODEL), 1.0),
        'x_sample': nrm(ks[1], (DEC_BATCH, DEC_SEQ, D_MODEL), 1.0),
        'mem_prompt': nrm(ks[2], (BATCH, MEM_LEN, D_MODEL), 1.0),
        'mem_sample': nrm(ks[3], (DEC_BATCH, MEM_LEN, D_MODEL), 1.0),
        'g_mix': gain(ks[4], (DEPTH, D_MODEL)),
        'w_in': nrm(ks[5], (DEPTH, D_MODEL, D_IN), D_MODEL ** -0.5),
        'sink_a': nrm(ks[6], (DEPTH, N_HEADS_A), 0.5),
        'g_out_a': gain(ks[7], (DEPTH, D_A)),
        'g_out_b': gain(ks[8], (DEPTH, D_B)),
        'w_out': nrm(ks[9], (DEPTH, D_MIX, D_MODEL), (2 * D_MIX) ** -0.5),
        'g_cross': gain(ks[10], (DEPTH, D_MODEL)),
        'g_mem': gain(ks[11], (DEPTH, D_MODEL)),
        'w_cq': nrm(ks[12], (DEPTH, D_MODEL, D_X), D_MODEL ** -0.5),
        'w_ckv': nrm(ks[13], (DEPTH, D_MODEL, 2 * D_X), D_MODEL ** -0.5),
        'w_co': nrm(ks[14], (DEPTH, D_X, D_MODEL), (2 * D_X) ** -0.5),
        'g_ffn': gain(ks[15], (DEPTH, D_MODEL)),
        'w_gu': nrm(ks[16], (DEPTH, D_MODEL, 2 * D_FF), D_MODEL ** -0.5),
        'conv_w': nrm(ks[17], (DEPTH, CONV_WIDTH, D_FF), CONV_WIDTH ** -0.5),
        'conv_b': nrm(ks[18], (DEPTH, D_FF), 0.02),
        'w_down': nrm(ks[19], (DEPTH, D_FF, D_MODEL), (2 * D_FF) ** -0.5),
        'g_final': gain(ks[20], (D_MODEL,)),
    }


def reference(x_prompt, x_sample, mem_prompt, mem_sample, g_mix, w_in, sink_a, g_out_a, g_out_b, w_out,
              g_cross, g_mem, w_cq, w_ckv, w_co, g_ffn, w_gu, conv_w, conv_b, w_down, g_final):
    y_prompt = trunk(x_prompt, mem_prompt, g_mix, w_in, sink_a, g_out_a, g_out_b, w_out, g_cross, g_mem,
                     w_cq, w_ckv, w_co, g_ffn, w_gu, conv_w, conv_b, w_down, g_final)
    y_sample = trunk(x_sample, mem_sample, g_mix, w_in, sink_a, g_out_a, g_out_b, w_out, g_cross, g_mem,
                     w_cq, w_ckv, w_co, g_ffn, w_gu, conv_w, conv_b, w_down, g_final)
    return (y_prompt, y_sample)
```

```python
import functools
import math

import jax
import jax.numpy as jnp
from jax import lax
from jax.experimental import pallas as pl
from jax.experimental.pallas import tpu as pltpu

D_MODEL = 1024
DEPTH = 2
HEAD_DIM = 64
N_HEADS_A = 8
N_KV_A = 2
N_HEADS_B = 8
D_A = N_HEADS_A * HEAD_DIM
D_KV_A = N_KV_A * HEAD_DIM
D_B = N_HEADS_B * HEAD_DIM
D_MIX = D_A + D_B
D_IN = D_A + 2 * D_KV_A + 3 * D_B
WINDOW_A = 128
DILATION_PAIRS = ((128, 1), (512, 4), (2048, 16))
N_HEADS_X = 4
HEAD_DIM_X = 128
D_X = N_HEADS_X * HEAD_DIM_X
D_FF = 2816
ROPE_THETA = 10000.0
EPS = 1e-6
NEG = -1e30

LANES = 128
HALO_ROWS = 16
VMEM_LIMIT = 48 * 1024 * 1024

F32 = jnp.float32
BF16 = jnp.bfloat16


def _params(n_axes):
    return pltpu.CompilerParams(
        dimension_semantics=("arbitrary",) * n_axes, vmem_limit_bytes=VMEM_LIMIT)


def _rms(xf, g):
    return xf * lax.rsqrt(jnp.mean(xf * xf, axis=-1, keepdims=True) + EPS) * g


def _lane_lo(shape):
    return lax.broadcasted_iota(jnp.int32, shape, 1) % LANES < HEAD_DIM


def _roll_heads(t):
    return pltpu.roll(t.astype(F32), HEAD_DIM, axis=1).astype(t.dtype)


def _proj_kernel(x_ref, g_ref, w_ref, cos_ref, sin_ref,
                 qa_ref, ka_ref, va_ref, qb_ref, kb_ref, vb_ref):
    h = _rms(x_ref[0], g_ref[...]).astype(BF16)
    cos = cos_ref[...]
    sin = sin_ref[...]
    first_half = lax.broadcasted_iota(jnp.int32, cos.shape, 1) % HEAD_DIM < HEAD_DIM // 2

    def rope(y, scale):
        partner = jnp.where(first_half,
                            pltpu.roll(y, LANES - HEAD_DIM // 2, axis=1),
                            pltpu.roll(y, HEAD_DIM // 2, axis=1))
        out = y * cos + partner * sin
        if scale != 1.0:
            out = out * scale
        return out.astype(BF16)

    def project(col0, width, out_ref, roped, scale=1.0):
        for c in range(width // LANES):
            lo = col0 + c * LANES
            y = jnp.dot(h, w_ref[:, lo:lo + LANES], preferred_element_type=F32)
            y = rope(y, scale) if roped else y.astype(BF16)
            out_ref[0, :, c * LANES:(c + 1) * LANES] = y

    q_scale = HEAD_DIM ** -0.5
    project(0, D_A, qa_ref, True, q_scale)
    project(D_A, D_KV_A, ka_ref, True)
    project(D_A + D_KV_A, D_KV_A, va_ref, False)
    off = D_A + 2 * D_KV_A
    project(off, D_B, qb_ref, True, q_scale)
    project(off + D_B, D_B, kb_ref, True)
    project(off + 2 * D_B, D_B, vb_ref, False)


def _in_projection(x, g, w_in, cos, sin, tm):
    n, s, _ = x.shape
    tok = lambda w: pl.BlockSpec((1, tm, w), lambda i, b: (b, i, 0))
    const = lambda shape: pl.BlockSpec(shape, lambda i, b: (0,) * len(shape))
    tab = pl.BlockSpec((tm, LANES), lambda i, b: (i, 0))
    widths = (D_A, D_KV_A, D_KV_A, D_B, D_B, D_B)
    return pl.pallas_call(
        _proj_kernel,
        grid=(s // tm, n),
        in_specs=[tok(D_MODEL), const((1, D_MODEL)), const((D_MODEL, D_IN)), tab, tab],
        out_specs=[tok(w) for w in widths],
        out_shape=[jax.ShapeDtypeStruct((n, s, w), BF16) for w in widths],
        compiler_params=_params(2),
        name="in_projection",
    )(x, g, w_in, cos, sin)


def _band_geometry(qi, tq, tk, seq, hw):
    q0 = qi * tq
    start = jnp.clip(q0 - hw, 0, seq - tk)
    start = pl.multiple_of(start, hw)
    qpos = q0 + lax.broadcasted_iota(jnp.int32, (tq, tk), 0)
    kpos = start + lax.broadcasted_iota(jnp.int32, (tq, tk), 1)
    return start, jnp.abs(qpos - kpos) <= hw


def _one_head(qm, k2, v2, band, sink):
    s = lax.dot_general(qm, k2, (((1,), (1,)), ((), ())), preferred_element_type=F32)
    s = jnp.where(band, s, NEG)
    m = jnp.max(s, axis=-1, keepdims=True)
    if sink is not None:
        m = jnp.maximum(m, sink)
    p = jnp.exp(s - m)
    den = jnp.sum(p, axis=-1, keepdims=True)
    if sink is not None:
        den = den + jnp.exp(sink - m)
    o = jnp.dot(p.astype(BF16), v2, preferred_element_type=F32)
    return o, m, den


def _attn_a_kernel(sink_ref, q_ref, k_ref, v_ref, g_ref, o_ref, *, tq, tk, seq, hw):
    start, band = _band_geometry(pl.program_id(1), tq, tk, seq, hw)
    k2 = k_ref[0, pl.ds(start, tk), :]
    v2 = v_ref[0, pl.ds(start, tk), :]
    v2r = _roll_heads(v2)
    lo = _lane_lo((tq, LANES))
    zero = jnp.zeros((tq, LANES), BF16)
    group = N_HEADS_A // N_KV_A
    tiles = []
    for pair in range(N_HEADS_A // 2):
        kv = (2 * pair) // group
        q2 = q_ref[0, :, pair * LANES:(pair + 1) * LANES]
        q2r = _roll_heads(q2)
        if kv == 0:
            qm_even, qm_odd = jnp.where(lo, q2, zero), jnp.where(lo, q2r, zero)
            v_even, v_odd = v2, v2r
        else:
            qm_even, qm_odd = jnp.where(lo, zero, q2r), jnp.where(lo, zero, q2)
            v_even, v_odd = v2r, v2
        oe, _, de = _one_head(qm_even, k2, v_even, band, sink_ref[2 * pair])
        oo, _, do = _one_head(qm_odd, k2, v_odd, band, sink_ref[2 * pair + 1])
        tiles.append(jnp.where(lo, oe * (1.0 / de), oo * (1.0 / do)))
    sumsq = sum(jnp.sum(t * t, axis=-1, keepdims=True) for t in tiles)
    inv = lax.rsqrt(sumsq * (1.0 / D_A) + EPS)
    for c, t in enumerate(tiles):
        o_ref[0, :, c * LANES:(c + 1) * LANES] = (
            t * inv * g_ref[:, c * LANES:(c + 1) * LANES]).astype(BF16)


def _attention_a(q, k, v, sink, g_oa, tq):
    n, s, _ = q.shape
    hw = WINDOW_A
    tk = min(tq + 2 * hw, s)
    kern = functools.partial(_attn_a_kernel, tq=tq, tk=tk, seq=s, hw=hw)
    return pl.pallas_call(
        kern,
        grid=(n, s // tq),
        in_specs=[
            pl.BlockSpec(memory_space=pltpu.SMEM),
            pl.BlockSpec((1, tq, D_A), lambda b, i: (b, i, 0)),
            pl.BlockSpec((1, s, D_KV_A), lambda b, i: (b, 0, 0)),
            pl.BlockSpec((1, s, D_KV_A), lambda b, i: (b, 0, 0)),
            pl.BlockSpec((1, D_A), lambda b, i: (0, 0)),
        ],
        out_specs=pl.BlockSpec((1, tq, D_A), lambda b, i: (b, i, 0)),
        out_shape=jax.ShapeDtypeStruct((n, s, D_A), BF16),
        compiler_params=_params(2),
        name="attention_a",
    )(sink, q, k, v, g_oa)


def _attn_b_kernel(q_ref, k_ref, v_ref, o_ref, lse_ref, *, tq, tk, seq, hw):
    start, band = _band_geometry(pl.program_id(1), tq, tk, seq, hw)
    lo = _lane_lo((tq, LANES))
    zero = jnp.zeros((tq, LANES), BF16)
    lane = lax.broadcasted_iota(jnp.int32, (tq, LANES), 1)
    lse = jnp.zeros((tq, LANES), F32)
    for pair in range(N_HEADS_B // 2):
        cols = slice(pair * LANES, (pair + 1) * LANES)
        q2 = q_ref[0, :, cols]
        k2 = k_ref[0, pl.ds(start, tk), cols]
        v2 = v_ref[0, pl.ds(start, tk), cols]
        oe, me, de = _one_head(jnp.where(lo, q2, zero), k2, v2, band, None)
        oo, mo, do = _one_head(jnp.where(lo, zero, q2), k2, v2, band, None)
        o_ref[0, :, cols] = jnp.where(lo, oe * (1.0 / de), oo * (1.0 / do))
        lse = jnp.where(lane == 2 * pair, me + jnp.log(de), lse)
        lse = jnp.where(lane == 2 * pair + 1, mo + jnp.log(do), lse)
    lse_ref[0] = lse


def _attention_b(q, k, v, hw, tq):
    g, seq, _ = q.shape
    tq = min(tq, seq)
    tk = min(tq + 2 * hw, seq)
    kern = functools.partial(_attn_b_kernel, tq=tq, tk=tk, seq=seq, hw=hw)
    return pl.pallas_call(
        kern,
        grid=(g, seq // tq),
        in_specs=[
            pl.BlockSpec((1, tq, D_B), lambda b, i: (b, i, 0)),
            pl.BlockSpec((1, seq, D_B), lambda b, i: (b, 0, 0)),
            pl.BlockSpec((1, seq, D_B), lambda b, i: (b, 0, 0)),
        ],
        out_specs=[
            pl.BlockSpec((1, tq, D_B), lambda b, i: (b, i, 0)),
            pl.BlockSpec((1, tq, LANES), lambda b, i: (b, i, 0)),
        ],
        out_shape=[
            jax.ShapeDtypeStruct((g, seq, D_B), F32),
            jax.ShapeDtypeStruct((g, seq, LANES), F32),
        ],
        compiler_params=_params(2),
        name="attention_b",
    )(q, k, v)


def _mix_out_kernel(x_ref, oa_ref, o1_ref, o2_ref, o3_ref, l1_ref, l2_ref, l3_ref,
                    g_ref, w_ref, out_ref):
    l1, l2, l3 = l1_ref[0], l2_ref[0], l3_ref[0]
    top = jnp.maximum(jnp.maximum(l1, l2), l3)
    e1, e2, e3 = jnp.exp(l1 - top), jnp.exp(l2 - top), jnp.exp(l3 - top)
    inv = 1.0 / (e1 + e2 + e3)
    w1, w2, w3 = e1 * inv, e2 * inv, e3 * inv
    tm = l1.shape[0]
    lo = _lane_lo((tm, LANES))
    tiles = []
    for c in range(D_B // LANES):
        cols = slice(c * LANES, (c + 1) * LANES)
        spread = lambda w: jnp.where(lo, w[:, 2 * c:2 * c + 1], w[:, 2 * c + 1:2 * c + 2])
        tiles.append(spread(w1) * o1_ref[0, :, cols] + spread(w2) * o2_ref[0, :, cols]
                     + spread(w3) * o3_ref[0, :, cols])
    sumsq = sum(jnp.sum(t * t, axis=-1, keepdims=True) for t in tiles)
    rinv = lax.rsqrt(sumsq * (1.0 / D_B) + EPS)
    acc = x_ref[0] + jnp.dot(oa_ref[0], w_ref[0:D_A, :], preferred_element_type=F32)
    for c, t in enumerate(tiles):
        cols = slice(c * LANES, (c + 1) * LANES)
        ob = (t * rinv * g_ref[:, cols]).astype(BF16)
        acc = acc + jnp.dot(ob, w_ref[D_A + c * LANES:D_A + (c + 1) * LANES, :],
                            preferred_element_type=F32)
    out_ref[0] = acc


def _mix_out(x, oa, obs, lses, g_ob, w_out, tm):
    n, s, _ = x.shape
    tok = lambda w: pl.BlockSpec((1, tm, w), lambda b, i: (b, i, 0))
    const = lambda shape: pl.BlockSpec(shape, lambda b, i: (0,) * len(shape))
    return pl.pallas_call(
        _mix_out_kernel,
        grid=(n, s // tm),
        in_specs=[tok(D_MODEL), tok(D_A)] + [tok(D_B)] * 3 + [tok(LANES)] * 3
                 + [const((1, D_B)), const((D_MIX, D_MODEL))],
        out_specs=tok(D_MODEL),
        out_shape=jax.ShapeDtypeStruct((n, s, D_MODEL), F32),
        compiler_params=_params(2),
        name="mix_out",
    )(x, oa, *obs, *lses, g_ob, w_out)


def _mem_kv_kernel(mem_ref, g_ref, w_ref, kv_ref):
    h = _rms(mem_ref[0], g_ref[...]).astype(BF16)
    kv_ref[0] = jnp.dot(h, w_ref[...], preferred_element_type=F32).astype(BF16)


def _mem_kv(mem, g_mem, w_ckv):
    n, m, _ = mem.shape
    return pl.pallas_call(
        _mem_kv_kernel,
        grid=(n,),
        in_specs=[pl.BlockSpec((1, m, D_MODEL), lambda b: (b, 0, 0)),
                  pl.BlockSpec((1, D_MODEL), lambda b: (0, 0)),
                  pl.BlockSpec((D_MODEL, 2 * D_X), lambda b: (0, 0))],
        out_specs=pl.BlockSpec((1, m, 2 * D_X), lambda b: (b, 0, 0)),
        out_shape=jax.ShapeDtypeStruct((n, m, 2 * D_X), BF16),
        compiler_params=_params(1),
        name="mem_kv",
    )(mem, g_mem, w_ckv)


def _cross_kernel(x_ref, g_ref, wq_ref, kv_ref, wo_ref, out_ref):
    x = x_ref[0]
    h = _rms(x, g_ref[...]).astype(BF16)
    q = jnp.dot(h, wq_ref[...], preferred_element_type=F32).astype(BF16)
    scale = HEAD_DIM_X ** -0.5
    acc = x
    for hd in range(N_HEADS_X):
        cols = slice(hd * HEAD_DIM_X, (hd + 1) * HEAD_DIM_X)
        k = kv_ref[0, :, cols]
        v = kv_ref[0, :, D_X + hd * HEAD_DIM_X:D_X + (hd + 1) * HEAD_DIM_X]
        s = lax.dot_general(q[:, cols], k, (((1,), (1,)), ((), ())),
                            preferred_element_type=F32) * scale
        p = jnp.exp(s - jnp.max(s, axis=-1, keepdims=True))
        inv = 1.0 / jnp.sum(p, axis=-1, keepdims=True)
        o = jnp.dot(p.astype(BF16), v, preferred_element_type=F32) * inv
        acc = acc + jnp.dot(o.astype(BF16), wo_ref[cols, :], preferred_element_type=F32)
    out_ref[0] = acc


def _cross_attention(x, kv, g_cross, w_cq, w_co, tm):
    n, s, _ = x.shape
    m = kv.shape[1]
    tok = pl.BlockSpec((1, tm, D_MODEL), lambda b, i: (b, i, 0))
    const = lambda shape: pl.BlockSpec(shape, lambda b, i: (0,) * len(shape))
    return pl.pallas_call(
        _cross_kernel,
        grid=(n, s // tm),
        in_specs=[tok, const((1, D_MODEL)), const((D_MODEL, D_X)),
                  pl.BlockSpec((1, m, 2 * D_X), lambda b, i: (b, 0, 0)),
                  const((D_X, D_MODEL))],
        out_specs=tok,
        out_shape=jax.ShapeDtypeStruct((n, s, D_MODEL), F32),
        compiler_params=_params(2),
        name="cross_attention",
    )(x, g_cross, w_cq, kv, w_co)


def _ffn_kernel(x_ref, xp_ref, xn_ref, g_ref, wg_ref, wu_ref, cw_ref, cb_ref, wd_ref,
                gf_ref, out_ref, h_ref, acc_ref, *, tm, final_norm):
    i = pl.program_id(1)
    j = pl.program_id(2)

    @pl.when(j == 0)
    def _():
        g = g_ref[...]
        h_ref[0:HALO_ROWS, :] = _rms(xp_ref[0], g).astype(BF16)
        h_ref[HALO_ROWS:HALO_ROWS + tm, :] = _rms(x_ref[0], g).astype(BF16)
        h_ref[HALO_ROWS + tm:, :] = _rms(xn_ref[0], g).astype(BF16)
        acc_ref[...] = x_ref[0]

    gate = jnp.dot(h_ref[...], wg_ref[...], preferred_element_type=F32)
    up = jnp.dot(h_ref[HALO_ROWS:HALO_ROWS + tm, :], wu_ref[...], preferred_element_type=F32)
    cur = gate[HALO_ROWS:HALO_ROWS + tm]
    halo = gate[0:HALO_ROWS]
    last = lax.broadcasted_iota(jnp.int32, halo.shape, 0) == HALO_ROWS - 1
    prev_row = jnp.sum(jnp.where(last, halo, 0.0), axis=0, keepdims=True)
    prev_row = jnp.where(i == 0, 0.0, prev_row)
    next_row = jnp.where(i == pl.num_programs(1) - 1, 0.0,
                         gate[HALO_ROWS + tm:HALO_ROWS + tm + 1])
    row = lax.broadcasted_iota(jnp.int32, cur.shape, 0)
    before = jnp.where(row == 0, prev_row, pltpu.roll(cur, 1, axis=0))
    after = jnp.where(row == tm - 1, next_row, pltpu.roll(cur, tm - 1, axis=0))
    conv = before * cw_ref[0:1, :] + cur * cw_ref[1:2, :] + after * cw_ref[2:3, :] + cb_ref[...]
    act = 0.5 * conv * (1.0 + lax.erf(conv * (2.0 ** -0.5))) * up
    acc_ref[...] += jnp.dot(act.astype(BF16), wd_ref[...], preferred_element_type=F32)

    @pl.when(j == pl.num_programs(2) - 1)
    def _():
        y = acc_ref[...]
        if final_norm:
            y = _rms(y, gf_ref[...])
        out_ref[0] = y


def _conv_ffn(x, g_ffn, w_gu, conv_w, conv_b, w_down, g_final, final_norm, tm, fc):
    n, s, _ = x.shape
    nf = D_FF // fc
    hb = tm // HALO_ROWS
    last_hb = s // HALO_ROWS - 1
    tok = pl.BlockSpec((1, tm, D_MODEL), lambda b, i, j: (b, i, 0))
    prev = pl.BlockSpec((1, HALO_ROWS, D_MODEL),
                        lambda b, i, j: (b, jnp.maximum(i * hb - 1, 0), 0))
    nxt = pl.BlockSpec((1, HALO_ROWS, D_MODEL),
                       lambda b, i, j: (b, jnp.minimum((i + 1) * hb, last_hb), 0))
    vec = pl.BlockSpec((1, D_MODEL), lambda b, i, j: (0, 0))
    kern = functools.partial(_ffn_kernel, tm=tm, final_norm=final_norm)
    return pl.pallas_call(
        kern,
        grid=(n, s // tm, nf),
        in_specs=[tok, prev, nxt, vec,
                  pl.BlockSpec((D_MODEL, fc), lambda b, i, j: (0, j)),
                  pl.BlockSpec((D_MODEL, fc), lambda b, i, j: (0, nf + j)),
                  pl.BlockSpec((3, fc), lambda b, i, j: (0, j)),
                  pl.BlockSpec((1, fc), lambda b, i, j: (0, j)),
                  pl.BlockSpec((fc, D_MODEL), lambda b, i, j: (j, 0)),
                  vec],
        out_specs=tok,
        out_shape=jax.ShapeDtypeStruct((n, s, D_MODEL), F32),
        scratch_shapes=[pltpu.VMEM((tm + 2 * HALO_ROWS, D_MODEL), BF16),
                        pltpu.VMEM((tm, D_MODEL), F32)],
        compiler_params=_params(3),
        name="conv_ffn",
    )(x, x, x, g_ffn, w_gu, w_gu, conv_w, conv_b, w_down, g_final)


def _rope_tables(seq):
    inv = 1.0 / (ROPE_THETA ** (jnp.arange(0, HEAD_DIM, 2, dtype=F32) / HEAD_DIM))
    ang = jnp.arange(seq, dtype=F32)[:, None] * inv[None, :]
    cos, sin = jnp.cos(ang), jnp.sin(ang)
    reps = LANES // HEAD_DIM
    return (jnp.tile(jnp.concatenate([cos, cos], axis=-1), (1, reps)),
            jnp.tile(jnp.concatenate([-sin, sin], axis=-1), (1, reps)))


def _to_sub(t, r):
    n, s, c = t.shape
    if r == 1:
        return t
    return t.reshape(n, s // r, r, c).transpose(0, 2, 1, 3).reshape(n * r, s // r, c)


def _from_sub(t, r, n):
    if r == 1:
        return t
    _, sub, c = t.shape
    return t.reshape(n, r, sub, c).transpose(0, 2, 1, 3).reshape(n, sub * r, c)


def _trunk(x, mem, p):
    n, s, _ = x.shape
    cos, sin = _rope_tables(s)
    row = lambda v: v.reshape(1, -1)
    for l in range(DEPTH):
        qa, ka, va, qb, kb, vb = _in_projection(
            x, row(p['g_mix'][l]), p['w_in'][l], cos, sin, tm=512)
        oa = _attention_a(qa, ka, va, p['sink_a'][l], row(p['g_out_a'][l]), tq=128)
        obs, lses = [], []
        for window, r in DILATION_PAIRS:
            o, lse = _attention_b(_to_sub(qb, r), _to_sub(kb, r), _to_sub(vb, r),
                                  hw=window // (2 * r), tq=128)
            obs.append(_from_sub(o, r, n))
            lses.append(_from_sub(lse, r, n))
        x = _mix_out(x, oa, obs, lses, row(p['g_out_b'][l]), p['w_out'][l], tm=512)
        kv = _mem_kv(mem, row(p['g_mem'][l]), p['w_ckv'][l])
        x = _cross_attention(x, kv, row(p['g_cross'][l]), p['w_cq'][l], p['w_co'][l], tm=512)
        x = _conv_ffn(x, row(p['g_ffn'][l]), p['w_gu'][l], p['conv_w'][l],
                      row(p['conv_b'][l]), p['w_down'][l], row(p['g_final']),
                      final_norm=(l == DEPTH - 1), tm=1024, fc=256)
    return x


def kernel(x_prompt, x_sample, mem_prompt, mem_sample, g_mix, w_in, sink_a, g_out_a, g_out_b, w_out, g_cross, g_mem, w_cq, w_ckv, w_co, g_ffn, w_gu, conv_w, conv_b, w_down, g_final):
    p = dict(g_mix=g_mix, sink_a=sink_a, g_out_a=g_out_a, g_out_b=g_out_b, g_cross=g_cross,
             g_mem=g_mem, g_ffn=g_ffn, conv_w=conv_w, conv_b=conv_b, g_final=g_final,
             w_in=w_in.astype(BF16), w_out=w_out.astype(BF16), w_cq=w_cq.astype(BF16),
             w_ckv=w_ckv.astype(BF16), w_co=w_co.astype(BF16), w_gu=w_gu.astype(BF16),
             w_down=w_down.astype(BF16))
    return (_trunk(x_prompt, mem_prompt, p), _trunk(x_sample, mem_sample, p))
```

```python
import functools

import jax
import jax.numpy as jnp
from jax import lax
from jax.experimental import pallas as pl
from jax.experimental.pallas import tpu as pltpu

D_MODEL = 1024
DEPTH = 2
HEAD_DIM = 64
N_HEADS_A = 8
N_KV_A = 2
N_HEADS_B = 8
D_A = N_HEADS_A * HEAD_DIM
D_KV_A = N_KV_A * HEAD_DIM
D_B = N_HEADS_B * HEAD_DIM
D_MIX = D_A + D_B
D_IN = D_A + 2 * D_KV_A + 3 * D_B
WINDOW_A = 128
DILATION_PAIRS = ((128, 1), (512, 4), (2048, 16))
N_HEADS_X = 4
HEAD_DIM_X = 128
D_X = N_HEADS_X * HEAD_DIM_X
D_FF = 2816
ROPE_THETA = 10000.0
EPS = 1e-6
NEG = -1e30

LANES = 128
HALO_ROWS = 16
N_SLABS = D_B // LANES
VMEM_LIMIT = 56 * 1024 * 1024

F32 = jnp.float32
BF16 = jnp.bfloat16


def _params(n_axes):
    return pltpu.CompilerParams(
        dimension_semantics=("arbitrary",) * n_axes, vmem_limit_bytes=VMEM_LIMIT)


def _rms(xf, g):
    return xf * lax.rsqrt(jnp.mean(xf * xf, axis=-1, keepdims=True) + EPS) * g


def _lane_lo(shape):
    return lax.broadcasted_iota(jnp.int32, shape, 1) % LANES < HEAD_DIM


def _roll_heads(t):
    return pltpu.roll(t.astype(F32), HEAD_DIM, axis=1).astype(t.dtype)


def _rows(start, size, stride):
    return pl.ds(start, size) if stride == 1 else pl.ds(start, size, stride=stride)


def _proj_kernel(x_ref, g_ref, w_ref, cos_ref, sin_ref,
                 qa_ref, ka_ref, va_ref, b1_ref, b4_ref, b16_ref, nat_ref, sub_ref, *, tm):
    h = _rms(x_ref[0], g_ref[...]).astype(BF16)
    cos = cos_ref[...]
    sin = sin_ref[...]
    first_half = lax.broadcasted_iota(jnp.int32, cos.shape, 1) % HEAD_DIM < HEAD_DIM // 2

    def rope(y, scale):
        partner = jnp.where(first_half,
                            pltpu.roll(y, LANES - HEAD_DIM // 2, axis=1),
                            pltpu.roll(y, HEAD_DIM // 2, axis=1))
        out = y * cos + partner * sin
        return out * scale if scale != 1.0 else out

    def project(col, width):
        return jnp.dot(h, w_ref[:, col:col + width], preferred_element_type=F32)

    q_scale = HEAD_DIM ** -0.5
    ya = project(0, D_A)
    for c in range(D_A // LANES):
        cols = slice(c * LANES, (c + 1) * LANES)
        qa_ref[0, :, cols] = rope(ya[:, cols], q_scale).astype(BF16)
    ykv = project(D_A, 2 * D_KV_A)
    ka_ref[0] = rope(ykv[:, 0:D_KV_A], 1.0).astype(BF16)
    va_ref[0] = ykv[:, D_KV_A:].astype(BF16)

    off = D_A + 2 * D_KV_A
    for part, (roped, scale) in enumerate(((True, q_scale), (True, 1.0), (False, 1.0))):
        yb = project(off + part * D_B, D_B)
        for c in range(N_SLABS):
            cols = slice(c * LANES, (c + 1) * LANES)
            y = rope(yb[:, cols], scale) if roped else yb[:, cols]
            b1_ref[part, 0, :, cols] = y.astype(BF16)
            nat_ref[...] = y
            for c0 in range(4):
                t4 = nat_ref[pl.ds(c0, tm // 4, stride=4), :]
                b4_ref[part, 0, c0, :, cols] = t4.astype(BF16)
                sub_ref[c0] = t4
            for c0 in range(4):
                for c1 in range(4):
                    t16 = sub_ref[c0, pl.ds(c1, tm // 16, stride=4), :]
                    b16_ref[part, 0, c0 + 4 * c1, :, cols] = t16.astype(BF16)


def _in_projection(x, g, w_in, cos, sin, tm):
    n, s, _ = x.shape
    tok = lambda w: pl.BlockSpec((1, tm, w), lambda i, b: (b, i, 0))
    const = lambda shape: pl.BlockSpec(shape, lambda i, b: (0,) * len(shape))
    tab = pl.BlockSpec((tm, LANES), lambda i, b: (i, 0))
    sub = lambda r: pl.BlockSpec((3, 1, r, tm // r, D_B), lambda i, b: (0, b, 0, i, 0))
    return pl.pallas_call(
        functools.partial(_proj_kernel, tm=tm),
        grid=(s // tm, n),
        in_specs=[tok(D_MODEL), const((1, D_MODEL)), const((D_MODEL, D_IN)), tab, tab],
        out_specs=[tok(D_A), tok(D_KV_A), tok(D_KV_A),
                   pl.BlockSpec((3, 1, tm, D_B), lambda i, b: (0, b, i, 0)), sub(4), sub(16)],
        out_shape=[jax.ShapeDtypeStruct((n, s, D_A), BF16),
                   jax.ShapeDtypeStruct((n, s, D_KV_A), BF16),
                   jax.ShapeDtypeStruct((n, s, D_KV_A), BF16),
                   jax.ShapeDtypeStruct((3, n, s, D_B), BF16),
                   jax.ShapeDtypeStruct((3, n, 4, s // 4, D_B), BF16),
                   jax.ShapeDtypeStruct((3, n, 16, s // 16, D_B), BF16)],
        scratch_shapes=[pltpu.VMEM((tm, LANES), F32), pltpu.VMEM((4, tm // 4, LANES), F32)],
        compiler_params=_params(2),
        name="in_projection",
    )(x, g, w_in, cos, sin)


def _band_geometry(qi, tq, tk, seq, hw, stack):
    q0 = qi * tq
    start = pl.multiple_of(jnp.clip(q0 - hw, 0, seq - tk), hw)
    shape = (stack * tq, tk)
    qpos = q0 + lax.broadcasted_iota(jnp.int32, shape, 0) % tq
    kpos = start + lax.broadcasted_iota(jnp.int32, shape, 1)
    return start, jnp.abs(qpos - kpos) <= hw


def _softmax_rows(s, band, sink):
    s = jnp.where(band, s, NEG)
    m = jnp.max(s, axis=-1, keepdims=True)
    if sink is not None:
        m = jnp.maximum(m, sink)
    p = jnp.exp(s - m)
    den = jnp.sum(p, axis=-1, keepdims=True)
    if sink is not None:
        den = den + jnp.exp(sink - m)
    return p.astype(BF16), m, den


def _qk(qs, k2):
    return lax.dot_general(qs, k2, (((1,), (1,)), ((), ())), preferred_element_type=F32)


def _attn_a_kernel(sink_ref, q_ref, k_ref, v_ref, g_ref, o_ref, *, tq, nsub, tk, seq, hw):
    for t in range(nsub):
        _attn_a_tile(sink_ref, q_ref, k_ref, v_ref, g_ref, o_ref,
                     pl.program_id(1) * nsub + t, slice(t * tq, (t + 1) * tq), tq, tk, seq, hw)


def _attn_a_tile(sink_ref, q_ref, k_ref, v_ref, g_ref, o_ref, qi, rows, tq, tk, seq, hw):
    group = N_HEADS_A // N_KV_A
    start, band = _band_geometry(qi, tq, tk, seq, hw, group)
    k2 = k_ref[0, pl.ds(start, tk), :]
    v2 = v_ref[0, pl.ds(start, tk), :]
    v2r = _roll_heads(v2)
    lo = _lane_lo((tq, LANES))
    zero = jnp.zeros((tq, LANES), BF16)
    block = lax.broadcasted_iota(jnp.int32, (group * tq, 1), 0) // tq
    tiles = [None] * (N_HEADS_A // 2)
    for kv in range(N_KV_A):
        pa, pb = 2 * kv, 2 * kv + 1
        qa = q_ref[0, rows, pa * LANES:(pa + 1) * LANES]
        qb = q_ref[0, rows, pb * LANES:(pb + 1) * LANES]
        qar, qbr = _roll_heads(qa), _roll_heads(qb)
        if kv == 0:
            parts = [jnp.where(lo, qa, zero), jnp.where(lo, qb, zero),
                     jnp.where(lo, qar, zero), jnp.where(lo, qbr, zero)]
            v_even, v_odd = v2, v2r
        else:
            parts = [jnp.where(lo, zero, qar), jnp.where(lo, zero, qbr),
                     jnp.where(lo, zero, qa), jnp.where(lo, zero, qb)]
            v_even, v_odd = v2r, v2
        heads = (2 * pa, 2 * pb, 2 * pa + 1, 2 * pb + 1)
        sink = jnp.zeros((group * tq, 1), F32)
        for blk, head in enumerate(heads):
            sink = jnp.where(block == blk, sink_ref[head], sink)
        p, _, den = _softmax_rows(_qk(jnp.concatenate(parts, axis=0), k2), band, sink)
        inv = 1.0 / den
        oe = jnp.dot(p[0:2 * tq], v_even, preferred_element_type=F32) * inv[0:2 * tq]
        oo = jnp.dot(p[2 * tq:], v_odd, preferred_element_type=F32) * inv[2 * tq:]
        tiles[pa] = jnp.where(lo, oe[0:tq], oo[0:tq])
        tiles[pb] = jnp.where(lo, oe[tq:], oo[tq:])
    sumsq = sum(jnp.sum(t * t, axis=-1, keepdims=True) for t in tiles)
    inv = lax.rsqrt(sumsq * (1.0 / D_A) + EPS)
    for c, t in enumerate(tiles):
        o_ref[0, rows, c * LANES:(c + 1) * LANES] = (
            t * inv * g_ref[:, c * LANES:(c + 1) * LANES]).astype(BF16)


def _attention_a(q, k, v, sink, g_oa, tq, nsub):
    n, s, _ = q.shape
    hw = WINDOW_A
    tk = min(tq + 2 * hw, s)
    rows = tq * nsub
    kern = functools.partial(_attn_a_kernel, tq=tq, nsub=nsub, tk=tk, seq=s, hw=hw)
    return pl.pallas_call(
        kern,
        grid=(n, s // rows),
        in_specs=[
            pl.BlockSpec(memory_space=pltpu.SMEM),
            pl.BlockSpec((1, rows, D_A), lambda b, i: (b, i, 0)),
            pl.BlockSpec((1, s, D_KV_A), lambda b, i: (b, 0, 0)),
            pl.BlockSpec((1, s, D_KV_A), lambda b, i: (b, 0, 0)),
            pl.BlockSpec((1, D_A), lambda b, i: (0, 0)),
        ],
        out_specs=pl.BlockSpec((1, rows, D_A), lambda b, i: (b, i, 0)),
        out_shape=jax.ShapeDtypeStruct((n, s, D_A), BF16),
        compiler_params=_params(2),
        name="attention_a",
    )(sink, q, k, v, g_oa)


B_UNROLL = 4


def _attn_b_kernel(q_ref, k_ref, v_ref, o_ref, lse_ref, *, r, tq, nsub, tk, seq, hw):
    lo = _lane_lo((tq, LANES))
    zero = jnp.zeros((tq, LANES), BF16)
    lane = lax.broadcasted_iota(jnp.int32, (tq, LANES), 1)
    geometry = [_band_geometry(pl.program_id(1) * nsub + t, tq, tk, seq, hw, 2)
                for t in range(nsub)]

    def unit(c, t):
        start, band = geometry[t]
        out_rows = _rows(r * t * tq + c, tq, r)
        lse = jnp.zeros((tq, LANES), F32)
        for pair in range(N_SLABS):
            cols = slice(pair * LANES, (pair + 1) * LANES)
            q2 = q_ref[0, 0, c, t * tq:(t + 1) * tq, cols]
            k2 = k_ref[0, 0, c, pl.ds(start, tk), cols]
            v2 = v_ref[0, 0, c, pl.ds(start, tk), cols]
            qs = jnp.concatenate([jnp.where(lo, q2, zero), jnp.where(lo, zero, q2)], axis=0)
            p, m, den = _softmax_rows(_qk(qs, k2), band, None)
            o = jnp.dot(p, v2, preferred_element_type=F32) * (1.0 / den)
            o_ref[0, pair, out_rows, :] = jnp.where(lo, o[0:tq], o[tq:])
            row_lse = m + jnp.log(den)
            lse = jnp.where(lane == 2 * pair, row_lse[0:tq], lse)
            lse = jnp.where(lane == 2 * pair + 1, row_lse[tq:], lse)
        lse_ref[0, out_rows, :] = lse

    if r * nsub <= B_UNROLL:
        for c in range(r):
            for t in range(nsub):
                unit(c, t)
    else:
        assert nsub == 1 and r % B_UNROLL == 0

        def body(grp, carry):
            for cc in range(B_UNROLL):
                unit(grp * B_UNROLL + cc, 0)
            return carry
        lax.fori_loop(0, r // B_UNROLL, body, 0)


def _attention_b(qkv, r, hw, tq, nsub):
    _, n, _, seq, _ = qkv.shape
    tq = min(tq, seq)
    tk = min(tq + 2 * hw, seq)
    rows = tq * nsub
    kern = functools.partial(_attn_b_kernel, r=r, tq=tq, nsub=nsub, tk=tk, seq=seq, hw=hw)
    full = lambda part: pl.BlockSpec((1, 1, r, seq, D_B), lambda b, i: (part, b, 0, 0, 0))
    return pl.pallas_call(
        kern,
        grid=(n, seq // rows),
        in_specs=[pl.BlockSpec((1, 1, r, rows, D_B), lambda b, i: (0, b, 0, i, 0)),
                  full(1), full(2)],
        out_specs=[pl.BlockSpec((1, N_SLABS, r * rows, LANES), lambda b, i: (b, 0, i, 0)),
                   pl.BlockSpec((1, r * rows, LANES), lambda b, i: (b, i, 0))],
        out_shape=[jax.ShapeDtypeStruct((n, N_SLABS, r * seq, LANES), F32),
                   jax.ShapeDtypeStruct((n, r * seq, LANES), F32)],
        compiler_params=_params(2),
        name=f"attention_b{r}",
    )(qkv, qkv, qkv)


def _mix_out_kernel(x_ref, oa_ref, o1_ref, o2_ref, o3_ref, l1_ref, l2_ref, l3_ref,
                    g_ref, w_ref, out_ref):
    l1, l2, l3 = l1_ref[0], l2_ref[0], l3_ref[0]
    top = jnp.maximum(jnp.maximum(l1, l2), l3)
    e1, e2, e3 = jnp.exp(l1 - top), jnp.exp(l2 - top), jnp.exp(l3 - top)
    inv = 1.0 / (e1 + e2 + e3)
    w1, w2, w3 = e1 * inv, e2 * inv, e3 * inv
    tm = l1.shape[0]
    lo = _lane_lo((tm, LANES))
    tiles = []
    for c in range(N_SLABS):
        spread = lambda w: jnp.where(lo, w[:, 2 * c:2 * c + 1], w[:, 2 * c + 1:2 * c + 2])
        tiles.append(spread(w1) * o1_ref[0, c] + spread(w2) * o2_ref[0, c]
                     + spread(w3) * o3_ref[0, c])
    sumsq = sum(jnp.sum(t * t, axis=-1, keepdims=True) for t in tiles)
    rinv = lax.rsqrt(sumsq * (1.0 / D_B) + EPS)
    acc = x_ref[0] + jnp.dot(oa_ref[0], w_ref[0:D_A, :], preferred_element_type=F32)
    for c, t in enumerate(tiles):
        cols = slice(c * LANES, (c + 1) * LANES)
        ob = (t * rinv * g_ref[:, cols]).astype(BF16)
        acc = acc + jnp.dot(ob, w_ref[D_A + c * LANES:D_A + (c + 1) * LANES, :],
                            preferred_element_type=F32)
    out_ref[0] = acc


def _mix_out(x, oa, obs, lses, g_ob, w_out, tm):
    n, s, _ = x.shape
    tok = lambda w: pl.BlockSpec((1, tm, w), lambda b, i: (b, i, 0))
    slab = pl.BlockSpec((1, N_SLABS, tm, LANES), lambda b, i: (b, 0, i, 0))
    const = lambda shape: pl.BlockSpec(shape, lambda b, i: (0,) * len(shape))
    return pl.pallas_call(
        _mix_out_kernel,
        grid=(n, s // tm),
        in_specs=[tok(D_MODEL), tok(D_A)] + [slab] * 3 + [tok(LANES)] * 3
                 + [const((1, D_B)), const((D_MIX, D_MODEL))],
        out_specs=tok(D_MODEL),
        out_shape=jax.ShapeDtypeStruct((n, s, D_MODEL), F32),
        compiler_params=_params(2),
        name="mix_out",
    )(x, oa, *obs, *lses, g_ob, w_out)


def _mem_kv_kernel(mem_ref, g_ref, w_ref, kv_ref):
    h = _rms(mem_ref[0], g_ref[...]).astype(BF16)
    kv_ref[0] = jnp.dot(h, w_ref[...], preferred_element_type=F32).astype(BF16)


def _mem_kv(mem, g_mem, w_ckv):
    n, m, _ = mem.shape
    return pl.pallas_call(
        _mem_kv_kernel,
        grid=(n,),
        in_specs=[pl.BlockSpec((1, m, D_MODEL), lambda b: (b, 0, 0)),
                  pl.BlockSpec((1, D_MODEL), lambda b: (0, 0)),
                  pl.BlockSpec((D_MODEL, 2 * D_X), lambda b: (0, 0))],
        out_specs=pl.BlockSpec((1, m, 2 * D_X), lambda b: (b, 0, 0)),
        out_shape=jax.ShapeDtypeStruct((n, m, 2 * D_X), BF16),
        compiler_params=_params(1),
        name="mem_kv",
    )(mem, g_mem, w_ckv)


def _cross_kernel(x_ref, g_ref, wq_ref, kv_ref, wo_ref, out_ref):
    x = x_ref[0]
    h = _rms(x, g_ref[...]).astype(BF16)
    q = jnp.dot(h, wq_ref[...], preferred_element_type=F32).astype(BF16)
    scale = HEAD_DIM_X ** -0.5
    acc = x
    for hd in range(N_HEADS_X):
        cols = slice(hd * HEAD_DIM_X, (hd + 1) * HEAD_DIM_X)
        k = kv_ref[0, :, cols]
        v = kv_ref[0, :, D_X + hd * HEAD_DIM_X:D_X + (hd + 1) * HEAD_DIM_X]
        s = _qk(q[:, cols], k) * scale
        p = jnp.exp(s - jnp.max(s, axis=-1, keepdims=True))
        inv = 1.0 / jnp.sum(p, axis=-1, keepdims=True)
        o = jnp.dot(p.astype(BF16), v, preferred_element_type=F32) * inv
        acc = acc + jnp.dot(o.astype(BF16), wo_ref[cols, :], preferred_element_type=F32)
    out_ref[0] = acc


def _cross_attention(x, kv, g_cross, w_cq, w_co, tm):
    n, s, _ = x.shape
    m = kv.shape[1]
    tok = pl.BlockSpec((1, tm, D_MODEL), lambda b, i: (b, i, 0))
    const = lambda shape: pl.BlockSpec(shape, lambda b, i: (0,) * len(shape))
    return pl.pallas_call(
        _cross_kernel,
        grid=(n, s // tm),
        in_specs=[tok, const((1, D_MODEL)), const((D_MODEL, D_X)),
                  pl.BlockSpec((1, m, 2 * D_X), lambda b, i: (b, 0, 0)),
                  const((D_X, D_MODEL))],
        out_specs=tok,
        out_shape=jax.ShapeDtypeStruct((n, s, D_MODEL), F32),
        compiler_params=_params(2),
        name="cross_attention",
    )(x, g_cross, w_cq, kv, w_co)


def _ffn_kernel(x_ref, xp_ref, xn_ref, g_ref, wgu_ref, cw_ref, cb_ref, wd_ref, gf_ref,
                out_ref, h_ref, act_ref, *, tm, fc, final_norm):
    i = pl.program_id(1)
    g = g_ref[...]
    h_ref[0:HALO_ROWS, :] = _rms(xp_ref[0], g).astype(BF16)
    h_ref[HALO_ROWS:HALO_ROWS + tm, :] = _rms(x_ref[0], g).astype(BF16)
    h_ref[HALO_ROWS + tm:, :] = _rms(xn_ref[0], g).astype(BF16)
    row = lax.broadcasted_iota(jnp.int32, (tm, fc), 0)
    last = lax.broadcasted_iota(jnp.int32, (HALO_ROWS, fc), 0) == HALO_ROWS - 1
    first_tile = i == 0
    last_tile = i == pl.num_programs(1) - 1
    for j in range(D_FF // fc):
        cols = slice(j * fc, (j + 1) * fc)
        gate = jnp.dot(h_ref[...], wgu_ref[:, cols], preferred_element_type=F32)
        up = jnp.dot(h_ref[HALO_ROWS:HALO_ROWS + tm, :], wgu_ref[:, D_FF + j * fc:D_FF + (j + 1) * fc],
                     preferred_element_type=F32)
        cur = gate[HALO_ROWS:HALO_ROWS + tm]
        prev_row = jnp.sum(jnp.where(last, gate[0:HALO_ROWS], 0.0), axis=0, keepdims=True)
        prev_row = jnp.where(first_tile, 0.0, prev_row)
        next_row = jnp.where(last_tile, 0.0, gate[HALO_ROWS + tm:HALO_ROWS + tm + 1])
        before = jnp.where(row == 0, prev_row, pltpu.roll(cur, 1, axis=0))
        after = jnp.where(row == tm - 1, next_row, pltpu.roll(cur, tm - 1, axis=0))
        conv = (before * cw_ref[0:1, cols] + cur * cw_ref[1:2, cols]
                + after * cw_ref[2:3, cols] + cb_ref[:, cols])
        act = 0.5 * conv * (1.0 + lax.erf(conv * (2.0 ** -0.5))) * up
        act_ref[:, cols] = act.astype(BF16)
    y = x_ref[0] + jnp.dot(act_ref[...], wd_ref[...], preferred_element_type=F32)
    if final_norm:
        y = _rms(y, gf_ref[...])
    out_ref[0] = y


def _conv_ffn(x, g_ffn, w_gu, conv_w, conv_b, w_down, g_final, final_norm, tm, fc):
    n, s, _ = x.shape
    hb = tm // HALO_ROWS
    last_hb = s // HALO_ROWS - 1
    tok = pl.BlockSpec((1, tm, D_MODEL), lambda b, i: (b, i, 0))
    prev = pl.BlockSpec((1, HALO_ROWS, D_MODEL), lambda b, i: (b, jnp.maximum(i * hb - 1, 0), 0))
    nxt = pl.BlockSpec((1, HALO_ROWS, D_MODEL),
                       lambda b, i: (b, jnp.minimum((i + 1) * hb, last_hb), 0))
    const = lambda shape: pl.BlockSpec(shape, lambda b, i: (0,) * len(shape))
    kern = functools.partial(_ffn_kernel, tm=tm, fc=fc, final_norm=final_norm)
    return pl.pallas_call(
        kern,
        grid=(n, s // tm),
        in_specs=[tok, prev, nxt, const((1, D_MODEL)), const((D_MODEL, 2 * D_FF)),
                  const((3, D_FF)), const((1, D_FF)), const((D_FF, D_MODEL)),
                  const((1, D_MODEL))],
        out_specs=tok,
        out_shape=jax.ShapeDtypeStruct((n, s, D_MODEL), F32),
        scratch_shapes=[pltpu.VMEM((tm + 2 * HALO_ROWS, D_MODEL), BF16),
                        pltpu.VMEM((tm, D_FF), BF16)],
        compiler_params=_params(2),
        name="conv_ffn",
    )(x, x, x, g_ffn, w_gu, conv_w, conv_b, w_down, g_final)


def _rope_tables(seq):
    inv = 1.0 / (ROPE_THETA ** (jnp.arange(0, HEAD_DIM, 2, dtype=F32) / HEAD_DIM))
    ang = jnp.arange(seq, dtype=F32)[:, None] * inv[None, :]
    cos, sin = jnp.cos(ang), jnp.sin(ang)
    reps = LANES // HEAD_DIM
    return (jnp.tile(jnp.concatenate([cos, cos], axis=-1), (1, reps)),
            jnp.tile(jnp.concatenate([-sin, sin], axis=-1), (1, reps)))


def _trunk(x, mem, p):
    n, s, _ = x.shape
    cos, sin = _rope_tables(s)
    row = lambda v: v.reshape(1, -1)
    for l in range(DEPTH):
        qa, ka, va, b1, b4, b16 = _in_projection(
            x, row(p['g_mix'][l]), p['w_in'][l], cos, sin, tm=512)
        oa = _attention_a(qa, ka, va, p['sink_a'][l], row(p['g_out_a'][l]), tq=128, nsub=2)
        subs = {1: b1.reshape(3, n, 1, s, D_B), 4: b4, 16: b16}
        tiling = {1: (128, 4), 4: (128, 1), 16: (128, 1)}
        obs, lses = [], []
        for window, r in DILATION_PAIRS:
            o, lse = _attention_b(subs[r], r, window // (2 * r), *tiling[r])
            obs.append(o)
            lses.append(lse)
        x = _mix_out(x, oa, obs, lses, row(p['g_out_b'][l]), p['w_out'][l], tm=512)
        kv = _mem_kv(mem, row(p['g_mem'][l]), p['w_ckv'][l])
        x = _cross_attention(x, kv, row(p['g_cross'][l]), p['w_cq'][l], p['w_co'][l], tm=512)
        x = _conv_ffn(x, row(p['g_ffn'][l]), p['w_gu'][l], p['conv_w'][l],
                      row(p['conv_b'][l]), p['w_down'][l], row(p['g_final']),
                      final_norm=(l == DEPTH - 1), tm=512, fc=256)
    return x


def kernel(x_prompt, x_sample, mem_prompt, mem_sample, g_mix, w_in, sink_a, g_out_a, g_out_b, w_out, g_cross, g_mem, w_cq, w_ckv, w_co, g_ffn, w_gu, conv_w, conv_b, w_down, g_final):
    p = dict(g_mix=g_mix, sink_a=sink_a, g_out_a=g_out_a, g_out_b=g_out_b, g_cross=g_cross,
             g_mem=g_mem, g_ffn=g_ffn, conv_w=conv_w, conv_b=conv_b, g_final=g_final,
             w_in=w_in.astype(BF16), w_out=w_out.astype(BF16), w_cq=w_cq.astype(BF16),
             w_ckv=w_ckv.astype(BF16), w_co=w_co.astype(BF16), w_gu=w_gu.astype(BF16),
             w_down=w_down.astype(BF16))
    return (_trunk(x_prompt, mem_prompt, p), _trunk(x_sample, mem_sample, p))
```

```python
import functools
import math

import numpy as np
import jax
import jax.numpy as jnp
from jax import lax
from jax.experimental import pallas as pl
from jax.experimental.pallas import tpu as pltpu

D_MODEL = 1024
DEPTH = 2
HEAD_DIM = 64
N_HEADS_A = 8
N_KV_A = 2
N_HEADS_B = 8
D_A = N_HEADS_A * HEAD_DIM
D_KV_A = N_KV_A * HEAD_DIM
D_B = N_HEADS_B * HEAD_DIM
D_MIX = D_A + D_B
D_IN = D_A + 2 * D_KV_A + 3 * D_B
WINDOW_A = 128
DILATION_PAIRS = ((128, 1), (512, 4), (2048, 16))
N_HEADS_X = 4
HEAD_DIM_X = 128
D_X = N_HEADS_X * HEAD_DIM_X
D_FF = 2816
ROPE_THETA = 10000.0
EPS = 1e-6
NEG = -1e30
LOG2E = math.log2(math.e)
LN2 = math.log(2.0)

LANES = 128
HALO_ROWS = 16
N_SLABS = D_B // LANES
VMEM_LIMIT = 56 * 1024 * 1024

F32 = jnp.float32
BF16 = jnp.bfloat16


def _params(n_axes):
    return pltpu.CompilerParams(
        dimension_semantics=("arbitrary",) * n_axes, vmem_limit_bytes=VMEM_LIMIT)


def _rms(xf, g):
    return xf * lax.rsqrt(jnp.mean(xf * xf, axis=-1, keepdims=True) + EPS) * g


def _lane_lo(shape):
    return lax.broadcasted_iota(jnp.int32, shape, 1) % LANES < HEAD_DIM


def _roll_heads(t):
    return pltpu.roll(t.astype(F32), HEAD_DIM, axis=1).astype(t.dtype)


def _rows(start, size, stride):
    return pl.ds(start, size) if stride == 1 else pl.ds(start, size, stride=stride)


def _proj_kernel(x_ref, g_ref, w_ref, cos_ref, sin_ref,
                 qa_ref, ka_ref, va_ref, b1_ref, b4_ref, b16_ref, nat_ref, sub_ref, *, tm):
    h = _rms(x_ref[0], g_ref[...]).astype(BF16)
    cos = cos_ref[...]
    sin = sin_ref[...]
    first_half = lax.broadcasted_iota(jnp.int32, cos.shape, 1) % HEAD_DIM < HEAD_DIM // 2

    def rope(y, scale):
        partner = jnp.where(first_half,
                            pltpu.roll(y, LANES - HEAD_DIM // 2, axis=1),
                            pltpu.roll(y, HEAD_DIM // 2, axis=1))
        out = y * cos + partner * sin
        return out * scale if scale != 1.0 else out

    def project(col, width):
        return jnp.dot(h, w_ref[:, col:col + width], preferred_element_type=F32)

    q_scale = HEAD_DIM ** -0.5 * LOG2E
    ya = project(0, D_A)
    for c in range(D_A // LANES):
        cols = slice(c * LANES, (c + 1) * LANES)
        qa_ref[0, :, cols] = rope(ya[:, cols], q_scale).astype(BF16)
    ykv = project(D_A, 2 * D_KV_A)
    ka_ref[0] = rope(ykv[:, 0:D_KV_A], 1.0).astype(BF16)
    va_ref[0] = ykv[:, D_KV_A:].astype(BF16)

    off = D_A + 2 * D_KV_A
    for part, (roped, scale) in enumerate(((True, q_scale), (True, 1.0), (False, 1.0))):
        yb = project(off + part * D_B, D_B)
        for c in range(N_SLABS):
            cols = slice(c * LANES, (c + 1) * LANES)
            y = rope(yb[:, cols], scale) if roped else yb[:, cols]
            b1_ref[part, 0, :, cols] = y.astype(BF16)
            nat_ref[...] = y
            for c0 in range(4):
                t4 = nat_ref[pl.ds(c0, tm // 4, stride=4), :]
                b4_ref[part, 0, c0, :, cols] = t4.astype(BF16)
                sub_ref[c0] = t4
            for c0 in range(4):
                for c1 in range(4):
                    t16 = sub_ref[c0, pl.ds(c1, tm // 16, stride=4), :]
                    b16_ref[part, 0, c0 + 4 * c1, :, cols] = t16.astype(BF16)


def _in_projection(x, g, w_in, cos, sin, tm):
    n, s, _ = x.shape
    tok = lambda w: pl.BlockSpec((1, tm, w), lambda i, b: (b, i, 0))
    const = lambda shape: pl.BlockSpec(shape, lambda i, b: (0,) * len(shape))
    tab = pl.BlockSpec((tm, LANES), lambda i, b: (i, 0))
    sub = lambda r: pl.BlockSpec((3, 1, r, tm // r, D_B), lambda i, b: (0, b, 0, i, 0))
    return pl.pallas_call(
        functools.partial(_proj_kernel, tm=tm),
        grid=(s // tm, n),
        in_specs=[tok(D_MODEL), const((1, D_MODEL)), const((D_MODEL, D_IN)), tab, tab],
        out_specs=[tok(D_A), tok(D_KV_A), tok(D_KV_A),
                   pl.BlockSpec((3, 1, tm, D_B), lambda i, b: (0, b, i, 0)), sub(4), sub(16)],
        out_shape=[jax.ShapeDtypeStruct((n, s, D_A), BF16),
                   jax.ShapeDtypeStruct((n, s, D_KV_A), BF16),
                   jax.ShapeDtypeStruct((n, s, D_KV_A), BF16),
                   jax.ShapeDtypeStruct((3, n, s, D_B), BF16),
                   jax.ShapeDtypeStruct((3, n, 4, s // 4, D_B), BF16),
                   jax.ShapeDtypeStruct((3, n, 16, s // 16, D_B), BF16)],
        scratch_shapes=[pltpu.VMEM((tm, LANES), F32), pltpu.VMEM((4, tm // 4, LANES), F32)],
        compiler_params=_params(2),
        name="in_projection",
    )(x, g, w_in, cos, sin)


def _window_start(qi, tq, tk, seq, hw):
    return min(max(qi * tq - hw, 0), seq - tk)


def _band_tables(tq, tk, seq, hw, stack):
    offsets = sorted({qi * tq - _window_start(qi, tq, tk, seq, hw) for qi in range(seq // tq)})
    row = np.arange(stack * tq)[:, None] % tq
    col = np.arange(tk)[None, :]
    tabs = [np.where(np.abs(row + off - col) <= hw, 0.0, NEG) for off in offsets]
    return tuple(offsets), jnp.asarray(np.stack(tabs), F32)


def _band_window(qi, tq, tk, seq, hw, offsets, bias_ref):
    q0 = qi * tq
    start = pl.multiple_of(jnp.clip(q0 - hw, 0, seq - tk), hw)
    case = 0
    for idx, off in enumerate(offsets):
        case = jnp.where(q0 - start == off, idx, case)
    return start, bias_ref[case]


def _softmax_rows(s, sink):
    m = jnp.max(s, axis=-1, keepdims=True)
    if sink is not None:
        m = jnp.maximum(m, sink)
    p = jnp.exp2(s - m)
    den = jnp.sum(p, axis=-1, keepdims=True)
    if sink is not None:
        den = den + jnp.exp2(sink - m)
    return p.astype(BF16), m, den


def _qk(qs, k2):
    return lax.dot_general(qs, k2, (((1,), (1,)), ((), ())), preferred_element_type=F32)


def _attn_a_kernel(sink_ref, q_ref, k_ref, v_ref, g_ref, bias_ref, o_ref,
                   *, tq, nsub, tk, seq, hw, offsets):
    for t in range(nsub):
        start, bias = _band_window(pl.program_id(1) * nsub + t, tq, tk, seq, hw, offsets, bias_ref)
        _attn_a_tile(sink_ref, q_ref, k_ref, v_ref, g_ref, o_ref, start, bias,
                     slice(t * tq, (t + 1) * tq), tq, tk)


def _attn_a_tile(sink_ref, q_ref, k_ref, v_ref, g_ref, o_ref, start, bias, rows, tq, tk):
    group = N_HEADS_A // N_KV_A
    k2 = k_ref[0, pl.ds(start, tk), :]
    v2 = v_ref[0, pl.ds(start, tk), :]
    v2r = _roll_heads(v2)
    lo = _lane_lo((tq, LANES))
    zero = jnp.zeros((tq, LANES), BF16)
    block = lax.broadcasted_iota(jnp.int32, (group * tq, 1), 0) // tq
    tiles = [None] * (N_HEADS_A // 2)
    for kv in range(N_KV_A):
        pa, pb = 2 * kv, 2 * kv + 1
        qa = q_ref[0, rows, pa * LANES:(pa + 1) * LANES]
        qb = q_ref[0, rows, pb * LANES:(pb + 1) * LANES]
        qar, qbr = _roll_heads(qa), _roll_heads(qb)
        if kv == 0:
            parts = [jnp.where(lo, qa, zero), jnp.where(lo, qb, zero),
                     jnp.where(lo, qar, zero), jnp.where(lo, qbr, zero)]
            v_even, v_odd = v2, v2r
        else:
            parts = [jnp.where(lo, zero, qar), jnp.where(lo, zero, qbr),
                     jnp.where(lo, zero, qa), jnp.where(lo, zero, qb)]
            v_even, v_odd = v2r, v2
        heads = (2 * pa, 2 * pb, 2 * pa + 1, 2 * pb + 1)
        sink = jnp.zeros((group * tq, 1), F32)
        for blk, head in enumerate(heads):
            sink = jnp.where(block == blk, sink_ref[head] * LOG2E, sink)
        p, _, den = _softmax_rows(_qk(jnp.concatenate(parts, axis=0), k2) + bias, sink)
        inv = 1.0 / den
        oe = jnp.dot(p[0:2 * tq], v_even, preferred_element_type=F32) * inv[0:2 * tq]
        oo = jnp.dot(p[2 * tq:], v_odd, preferred_element_type=F32) * inv[2 * tq:]
        tiles[pa] = jnp.where(lo, oe[0:tq], oo[0:tq])
        tiles[pb] = jnp.where(lo, oe[tq:], oo[tq:])
    sumsq = sum(jnp.sum(t * t, axis=-1, keepdims=True) for t in tiles)
    inv = lax.rsqrt(sumsq * (1.0 / D_A) + EPS)
    for c, t in enumerate(tiles):
        o_ref[0, rows, c * LANES:(c + 1) * LANES] = (
            t * inv * g_ref[:, c * LANES:(c + 1) * LANES]).astype(BF16)


def _attention_a(q, k, v, sink, g_oa, tq, nsub):
    n, s, _ = q.shape
    hw = WINDOW_A
    tk = min(tq + 2 * hw, s)
    rows = tq * nsub
    offsets, bias = _band_tables(tq, tk, s, hw, N_HEADS_A // N_KV_A)
    kern = functools.partial(_attn_a_kernel, tq=tq, nsub=nsub, tk=tk, seq=s, hw=hw,
                             offsets=offsets)
    return pl.pallas_call(
        kern,
        grid=(n, s // rows),
        in_specs=[
            pl.BlockSpec(memory_space=pltpu.SMEM),
            pl.BlockSpec((1, rows, D_A), lambda b, i: (b, i, 0)),
            pl.BlockSpec((1, s, D_KV_A), lambda b, i: (b, 0, 0)),
            pl.BlockSpec((1, s, D_KV_A), lambda b, i: (b, 0, 0)),
            pl.BlockSpec((1, D_A), lambda b, i: (0, 0)),
            pl.BlockSpec(bias.shape, lambda b, i: (0, 0, 0)),
        ],
        out_specs=pl.BlockSpec((1, rows, D_A), lambda b, i: (b, i, 0)),
        out_shape=jax.ShapeDtypeStruct((n, s, D_A), BF16),
        compiler_params=_params(2),
        name="attention_a",
    )(sink, q, k, v, g_oa, bias)


B_UNROLL = 4


def _attn_b_kernel(q_ref, k_ref, v_ref, bias_ref, o_ref, lse_ref,
                   *, r, tq, nsub, tk, seq, hw, offsets):
    lo = _lane_lo((tq, LANES))
    zero = jnp.zeros((tq, LANES), BF16)

    def unit(c, t):
        start, bias = _band_window(pl.program_id(1) * nsub + t, tq, tk, seq, hw, offsets, bias_ref)
        out_rows = _rows(r * t * tq + c, tq, r)
        for pair in range(N_SLABS):
            cols = slice(pair * LANES, (pair + 1) * LANES)
            q2 = q_ref[0, 0, c, t * tq:(t + 1) * tq, cols]
            k2 = k_ref[0, 0, c, pl.ds(start, tk), cols]
            v2 = v_ref[0, 0, c, pl.ds(start, tk), cols]
            qs = jnp.concatenate([jnp.where(lo, q2, zero), jnp.where(lo, zero, q2)], axis=0)
            p, m, den = _softmax_rows(_qk(qs, k2) + bias, None)
            o = jnp.dot(p, v2, preferred_element_type=F32) * (1.0 / den)
            o_ref[0, pair, out_rows, :] = jnp.where(lo, o[0:tq], o[tq:])
            row_lse = m * LN2 + jnp.log(den)
            lse_ref[0, pair, out_rows, :] = jnp.where(lo, row_lse[0:tq], row_lse[tq:])

    if r * nsub <= B_UNROLL:
        for c in range(r):
            for t in range(nsub):
                unit(c, t)
    else:
        assert nsub == 1 and r % B_UNROLL == 0

        def body(grp, carry):
            for cc in range(B_UNROLL):
                unit(grp * B_UNROLL + cc, 0)
            return carry
        lax.fori_loop(0, r // B_UNROLL, body, 0)


def _attention_b(qkv, r, hw, tq, nsub):
    _, n, _, seq, _ = qkv.shape
    tq = min(tq, seq)
    tk = min(tq + 2 * hw, seq)
    rows = tq * nsub
    offsets, bias = _band_tables(tq, tk, seq, hw, 2)
    kern = functools.partial(_attn_b_kernel, r=r, tq=tq, nsub=nsub, tk=tk, seq=seq, hw=hw,
                             offsets=offsets)
    full = lambda part: pl.BlockSpec((1, 1, r, seq, D_B), lambda b, i: (part, b, 0, 0, 0))
    return pl.pallas_call(
        kern,
        grid=(n, seq // rows),
        in_specs=[pl.BlockSpec((1, 1, r, rows, D_B), lambda b, i: (0, b, 0, i, 0)),
                  full(1), full(2), pl.BlockSpec(bias.shape, lambda b, i: (0, 0, 0))],
        out_specs=[pl.BlockSpec((1, N_SLABS, r * rows, LANES), lambda b, i: (b, 0, i, 0))] * 2,
        out_shape=[jax.ShapeDtypeStruct((n, N_SLABS, r * seq, LANES), F32)] * 2,
        compiler_params=_params(2),
        name=f"attention_b{r}",
    )(qkv, qkv, qkv, bias)


def _mem_kv_kernel(mem_ref, g_ref, w_ref, kv_ref):
    h = _rms(mem_ref[0], g_ref[...]).astype(BF16)
    kv_ref[0] = jnp.dot(h, w_ref[...], preferred_element_type=F32).astype(BF16)


def _mem_kv(mem, g_mem, w_ckv):
    n, m, _ = mem.shape
    return pl.pallas_call(
        _mem_kv_kernel,
        grid=(n,),
        in_specs=[pl.BlockSpec((1, m, D_MODEL), lambda b: (b, 0, 0)),
                  pl.BlockSpec((1, D_MODEL), lambda b: (0, 0)),
                  pl.BlockSpec((D_MODEL, 2 * D_X), lambda b: (0, 0))],
        out_specs=pl.BlockSpec((1, m, 2 * D_X), lambda b: (b, 0, 0)),
        out_shape=jax.ShapeDtypeStruct((n, m, 2 * D_X), BF16),
        compiler_params=_params(1),
        name="mem_kv",
    )(mem, g_mem, w_ckv)


def _mix_rows(rows, x_ref, oa_ref, o_refs, l_refs, g_ref, w_ref):
    tiles = []
    for c in range(N_SLABS):
        l1, l2, l3 = (l_ref[0, c, rows, :] for l_ref in l_refs)
        top = jnp.maximum(jnp.maximum(l1, l2), l3)
        es = [jnp.exp(l - top) for l in (l1, l2, l3)]
        inv = 1.0 / (es[0] + es[1] + es[2])
        tiles.append(sum(e * o_ref[0, c, rows, :] for e, o_ref in zip(es, o_refs)) * inv)
    sumsq = sum(jnp.sum(t * t, axis=-1, keepdims=True) for t in tiles)
    rinv = lax.rsqrt(sumsq * (1.0 / D_B) + EPS)
    ob = jnp.concatenate(
        [(t * rinv * g_ref[:, c * LANES:(c + 1) * LANES]).astype(BF16)
         for c, t in enumerate(tiles)], axis=1)
    mix = jnp.concatenate([oa_ref[0, rows, :], ob], axis=1)
    return x_ref[0, rows, :] + jnp.dot(mix, w_ref[...], preferred_element_type=F32)


def _cross_rows(x, g_ref, wq_ref, kv_ref, wo_ref):
    h = _rms(x, g_ref[...]).astype(BF16)
    scale = HEAD_DIM_X ** -0.5 * LOG2E
    q = (jnp.dot(h, wq_ref[...], preferred_element_type=F32) * scale).astype(BF16)
    heads = []
    for hd in range(N_HEADS_X):
        cols = slice(hd * HEAD_DIM_X, (hd + 1) * HEAD_DIM_X)
        k = kv_ref[0, :, cols]
        v = kv_ref[0, :, D_X + hd * HEAD_DIM_X:D_X + (hd + 1) * HEAD_DIM_X]
        p, _, den = _softmax_rows(_qk(q[:, cols], k), None)
        heads.append((jnp.dot(p, v, preferred_element_type=F32) * (1.0 / den)).astype(BF16))
    return x + jnp.dot(jnp.concatenate(heads, axis=1), wo_ref[...], preferred_element_type=F32)


def _post_mixer_kernel(x_ref, oa_ref, o1_ref, o2_ref, o3_ref, l1_ref, l2_ref, l3_ref,
                       gob_ref, wout_ref, gx_ref, wq_ref, kv_ref, wo_ref, out_ref, *, ts, nsub):
    for t in range(nsub):
        rows = slice(t * ts, (t + 1) * ts)
        x1 = _mix_rows(rows, x_ref, oa_ref, (o1_ref, o2_ref, o3_ref),
                       (l1_ref, l2_ref, l3_ref), gob_ref, wout_ref)
        out_ref[0, rows, :] = _cross_rows(x1, gx_ref, wq_ref, kv_ref, wo_ref)


def _post_mixer(x, oa, obs, lses, g_ob, w_out, kv, g_cross, w_cq, w_co, ts, nsub):
    n, s, _ = x.shape
    m = kv.shape[1]
    tm = ts * nsub
    tok = lambda w: pl.BlockSpec((1, tm, w), lambda b, i: (b, i, 0))
    slab = pl.BlockSpec((1, N_SLABS, tm, LANES), lambda b, i: (b, 0, i, 0))
    const = lambda shape: pl.BlockSpec(shape, lambda b, i: (0,) * len(shape))
    return pl.pallas_call(
        functools.partial(_post_mixer_kernel, ts=ts, nsub=nsub),
        grid=(n, s // tm),
        in_specs=[tok(D_MODEL), tok(D_A)] + [slab] * 6
                 + [const((1, D_B)), const((D_MIX, D_MODEL)), const((1, D_MODEL)),
                    const((D_MODEL, D_X)),
                    pl.BlockSpec((1, m, 2 * D_X), lambda b, i: (b, 0, 0)),
                    const((D_X, D_MODEL))],
        out_specs=tok(D_MODEL),
        out_shape=jax.ShapeDtypeStruct((n, s, D_MODEL), F32),
        compiler_params=_params(2),
        name="post_mixer",
    )(x, oa, *obs, *lses, g_ob, w_out, g_cross, w_cq, kv, w_co)


def _ffn_kernel(x_ref, xp_ref, xn_ref, g_ref, wgu_ref, cw_ref, cb_ref, wd_ref, gf_ref,
                out_ref, h_ref, act_ref, *, tm, fc, final_norm):
    i = pl.program_id(1)
    g = g_ref[...]
    h_ref[0:HALO_ROWS, :] = _rms(xp_ref[0], g).astype(BF16)
    h_ref[HALO_ROWS:HALO_ROWS + tm, :] = _rms(x_ref[0], g).astype(BF16)
    h_ref[HALO_ROWS + tm:, :] = _rms(xn_ref[0], g).astype(BF16)
    row = lax.broadcasted_iota(jnp.int32, (tm, fc), 0)
    last = lax.broadcasted_iota(jnp.int32, (HALO_ROWS, fc), 0) == HALO_ROWS - 1
    first_tile = i == 0
    last_tile = i == pl.num_programs(1) - 1
    for j in range(D_FF // fc):
        cols = slice(j * fc, (j + 1) * fc)
        gate = jnp.dot(h_ref[...], wgu_ref[:, cols], preferred_element_type=F32)
        up = jnp.dot(h_ref[HALO_ROWS:HALO_ROWS + tm, :], wgu_ref[:, D_FF + j * fc:D_FF + (j + 1) * fc],
                     preferred_element_type=F32)
        cur = gate[HALO_ROWS:HALO_ROWS + tm]
        prev_row = jnp.sum(jnp.where(last, gate[0:HALO_ROWS], 0.0), axis=0, keepdims=True)
        prev_row = jnp.where(first_tile, 0.0, prev_row)
        next_row = jnp.where(last_tile, 0.0, gate[HALO_ROWS + tm:HALO_ROWS + tm + 1])
        before = jnp.where(row == 0, prev_row, pltpu.roll(cur, 1, axis=0))
        after = jnp.where(row == tm - 1, next_row, pltpu.roll(cur, tm - 1, axis=0))
        conv = (before * cw_ref[0:1, cols] + cur * cw_ref[1:2, cols]
                + after * cw_ref[2:3, cols] + cb_ref[:, cols])
        act = 0.5 * conv * (1.0 + lax.erf(conv * (2.0 ** -0.5))) * up
        act_ref[:, cols] = act.astype(BF16)
    y = x_ref[0] + jnp.dot(act_ref[...], wd_ref[...], preferred_element_type=F32)
    if final_norm:
        y = _rms(y, gf_ref[...])
    out_ref[0] = y


def _conv_ffn(x, g_ffn, w_gu, conv_w, conv_b, w_down, g_final, final_norm, tm, fc):
    n, s, _ = x.shape
    hb = tm // HALO_ROWS
    last_hb = s // HALO_ROWS - 1
    tok = pl.BlockSpec((1, tm, D_MODEL), lambda b, i: (b, i, 0))
    prev = pl.BlockSpec((1, HALO_ROWS, D_MODEL), lambda b, i: (b, jnp.maximum(i * hb - 1, 0), 0))
    nxt = pl.BlockSpec((1, HALO_ROWS, D_MODEL),
                       lambda b, i: (b, jnp.minimum((i + 1) * hb, last_hb), 0))
    const = lambda shape: pl.BlockSpec(shape, lambda b, i: (0,) * len(shape))
    kern = functools.partial(_ffn_kernel, tm=tm, fc=fc, final_norm=final_norm)
    return pl.pallas_call(
        kern,
        grid=(n, s // tm),
        in_specs=[tok, prev, nxt, const((1, D_MODEL)), const((D_MODEL, 2 * D_FF)),
                  const((3, D_FF)), const((1, D_FF)), const((D_FF, D_MODEL)),
                  const((1, D_MODEL))],
        out_specs=tok,
        out_shape=jax.ShapeDtypeStruct((n, s, D_MODEL), F32),
        scratch_shapes=[pltpu.VMEM((tm + 2 * HALO_ROWS, D_MODEL), BF16),
                        pltpu.VMEM((tm, D_FF), BF16)],
        compiler_params=_params(2),
        name="conv_ffn",
    )(x, x, x, g_ffn, w_gu, conv_w, conv_b, w_down, g_final)


def _rope_tables(seq):
    inv = 1.0 / (ROPE_THETA ** (jnp.arange(0, HEAD_DIM, 2, dtype=F32) / HEAD_DIM))
    ang = jnp.arange(seq, dtype=F32)[:, None] * inv[None, :]
    cos, sin = jnp.cos(ang), jnp.sin(ang)
    reps = LANES // HEAD_DIM
    return (jnp.tile(jnp.concatenate([cos, cos], axis=-1), (1, reps)),
            jnp.tile(jnp.concatenate([-sin, sin], axis=-1), (1, reps)))


def _trunk(x, mem, p):
    n, s, _ = x.shape
    cos, sin = _rope_tables(s)
    row = lambda v: v.reshape(1, -1)
    for l in range(DEPTH):
        qa, ka, va, b1, b4, b16 = _in_projection(
            x, row(p['g_mix'][l]), p['w_in'][l], cos, sin, tm=512)
        oa = _attention_a(qa, ka, va, p['sink_a'][l], row(p['g_out_a'][l]), tq=128, nsub=4)
        subs = {1: b1.reshape(3, n, 1, s, D_B), 4: b4, 16: b16}
        tiling = {1: (128, 4), 4: (128, 1), 16: (128, 1)}
        obs, lses = [], []
        for window, r in DILATION_PAIRS:
            o, lse = _attention_b(subs[r], r, window // (2 * r), *tiling[r])
            obs.append(o)
            lses.append(lse)
        kv = _mem_kv(mem, row(p['g_mem'][l]), p['w_ckv'][l])
        x = _post_mixer(x, oa, obs, lses, row(p['g_out_b'][l]), p['w_out'][l], kv,
                        row(p['g_cross'][l]), p['w_cq'][l], p['w_co'][l], ts=256, nsub=2)
        x = _conv_ffn(x, row(p['g_ffn'][l]), p['w_gu'][l], p['conv_w'][l],
                      row(p['conv_b'][l]), p['w_down'][l], row(p['g_final']),
                      final_norm=(l == DEPTH - 1), tm=512, fc=256)
    return x


def kernel(x_prompt, x_sample, mem_prompt, mem_sample, g_mix, w_in, sink_a, g_out_a, g_out_b, w_out, g_cross, g_mem, w_cq, w_ckv, w_co, g_ffn, w_gu, conv_w, conv_b, w_down, g_final):
    p = dict(g_mix=g_mix, sink_a=sink_a, g_out_a=g_out_a, g_out_b=g_out_b, g_cross=g_cross,
             g_mem=g_mem, g_ffn=g_ffn, conv_w=conv_w, conv_b=conv_b, g_final=g_final,
             w_in=w_in.astype(BF16), w_out=w_out.astype(BF16), w_cq=w_cq.astype(BF16),
             w_ckv=w_ckv.astype(BF16), w_co=w_co.astype(BF16), w_gu=w_gu.astype(BF16),
             w_down=w_down.astype(BF16))
    return (_trunk(x_prompt, mem_prompt, p), _trunk(x_sample, mem_sample, p))
```

```python
import functools
import math

import numpy as np
import jax
import jax.numpy as jnp
from jax import lax
from jax.experimental import pallas as pl
from jax.experimental.pallas import tpu as pltpu

D_MODEL = 1024
DEPTH = 2
HEAD_DIM = 64
N_HEADS_A = 8
N_KV_A = 2
N_HEADS_B = 8
D_A = N_HEADS_A * HEAD_DIM
D_KV_A = N_KV_A * HEAD_DIM
D_B = N_HEADS_B * HEAD_DIM
D_MIX = D_A + D_B
D_IN = D_A + 2 * D_KV_A + 3 * D_B
WINDOW_A = 128
DILATION_PAIRS = ((128, 1), (512, 4), (2048, 16))
N_HEADS_X = 4
HEAD_DIM_X = 128
D_X = N_HEADS_X * HEAD_DIM_X
D_FF = 2816
ROPE_THETA = 10000.0
EPS = 1e-6
NEG = -1e30
LOG2E = math.log2(math.e)

LANES = 128
HALO_ROWS = 16
N_SLABS = D_B // LANES
VMEM_LIMIT = 56 * 1024 * 1024

F32 = jnp.float32
BF16 = jnp.bfloat16


def _params(n_axes):
    return pltpu.CompilerParams(
        dimension_semantics=("arbitrary",) * n_axes, vmem_limit_bytes=VMEM_LIMIT)


def _rms(xf, g):
    return xf * lax.rsqrt(jnp.mean(xf * xf, axis=-1, keepdims=True) + EPS) * g


def _lane_lo(shape):
    return lax.broadcasted_iota(jnp.int32, shape, 1) % LANES < HEAD_DIM


def _roll_heads(t):
    return pltpu.roll(t.astype(F32), HEAD_DIM, axis=1).astype(t.dtype)


def _proj_kernel(x_ref, g_ref, w_ref, cos_ref, sin_ref,
                 qa_ref, ka_ref, va_ref, b1_ref, b4_ref, b16_ref, nat_ref, sub_ref, *, tm):
    h = _rms(x_ref[0], g_ref[...]).astype(BF16)
    cos = cos_ref[...]
    sin = sin_ref[...]
    first_half = lax.broadcasted_iota(jnp.int32, cos.shape, 1) % HEAD_DIM < HEAD_DIM // 2

    def rope(y, scale):
        partner = jnp.where(first_half,
                            pltpu.roll(y, LANES - HEAD_DIM // 2, axis=1),
                            pltpu.roll(y, HEAD_DIM // 2, axis=1))
        out = y * cos + partner * sin
        return out * scale if scale != 1.0 else out

    def project(col, width):
        return jnp.dot(h, w_ref[:, col:col + width], preferred_element_type=F32)

    q_scale = HEAD_DIM ** -0.5 * LOG2E
    ya = project(0, D_A)
    for c in range(D_A // LANES):
        cols = slice(c * LANES, (c + 1) * LANES)
        qa_ref[0, :, cols] = rope(ya[:, cols], q_scale).astype(BF16)
    ykv = project(D_A, 2 * D_KV_A)
    ka_ref[0] = rope(ykv[:, 0:D_KV_A], 1.0).astype(BF16)
    va_ref[0] = ykv[:, D_KV_A:].astype(BF16)

    off = D_A + 2 * D_KV_A
    for part, (roped, scale) in enumerate(((True, q_scale), (True, 1.0), (False, 1.0))):
        yb = project(off + part * D_B, D_B)
        for c in range(N_SLABS):
            cols = slice(c * LANES, (c + 1) * LANES)
            y = rope(yb[:, cols], scale) if roped else yb[:, cols]
            b1_ref[part, 0, :, cols] = y.astype(BF16)
            nat_ref[...] = y
            for c0 in range(4):
                t4 = nat_ref[pl.ds(c0, tm // 4, stride=4), :]
                b4_ref[part, 0, c0, :, cols] = t4.astype(BF16)
                sub_ref[c0] = t4
            for c0 in range(4):
                for c1 in range(4):
                    t16 = sub_ref[c0, pl.ds(c1, tm // 16, stride=4), :]
                    b16_ref[part, 0, c0 + 4 * c1, :, cols] = t16.astype(BF16)


def _in_projection(x, g, w_in, cos, sin, tm):
    n, s, _ = x.shape
    tok = lambda w: pl.BlockSpec((1, tm, w), lambda i, b: (b, i, 0))
    const = lambda shape: pl.BlockSpec(shape, lambda i, b: (0,) * len(shape))
    tab = pl.BlockSpec((tm, LANES), lambda i, b: (i, 0))
    sub = lambda r: pl.BlockSpec((3, 1, r, tm // r, D_B), lambda i, b: (0, b, 0, i, 0))
    return pl.pallas_call(
        functools.partial(_proj_kernel, tm=tm),
        grid=(s // tm, n),
        in_specs=[tok(D_MODEL), const((1, D_MODEL)), const((D_MODEL, D_IN)), tab, tab],
        out_specs=[tok(D_A), tok(D_KV_A), tok(D_KV_A),
                   pl.BlockSpec((3, 1, tm, D_B), lambda i, b: (0, b, i, 0)), sub(4), sub(16)],
        out_shape=[jax.ShapeDtypeStruct((n, s, D_A), BF16),
                   jax.ShapeDtypeStruct((n, s, D_KV_A), BF16),
                   jax.ShapeDtypeStruct((n, s, D_KV_A), BF16),
                   jax.ShapeDtypeStruct((3, n, s, D_B), BF16),
                   jax.ShapeDtypeStruct((3, n, 4, s // 4, D_B), BF16),
                   jax.ShapeDtypeStruct((3, n, 16, s // 16, D_B), BF16)],
        scratch_shapes=[pltpu.VMEM((tm, LANES), F32), pltpu.VMEM((4, tm // 4, LANES), F32)],
        compiler_params=_params(2),
        name="in_projection",
    )(x, g, w_in, cos, sin)


def _window_start(qi, tq, tk, seq, hw):
    return min(max(qi * tq - hw, 0), seq - tk)


def _band_tables(tq, tk, seq, hw, stack):
    offsets = sorted({qi * tq - _window_start(qi, tq, tk, seq, hw) for qi in range(seq // tq)})
    row = np.arange(stack * tq)[:, None] % tq
    col = np.arange(tk)[None, :]
    tabs = [np.where(np.abs(row + off - col) <= hw, 0.0, NEG) for off in offsets]
    return tuple(offsets), jnp.asarray(np.stack(tabs), F32)


def _band_window(qi, tq, tk, seq, hw, offsets, bias_ref):
    q0 = qi * tq
    start = pl.multiple_of(jnp.clip(q0 - hw, 0, seq - tk), hw)
    case = 0
    for idx, off in enumerate(offsets):
        case = jnp.where(q0 - start == off, idx, case)
    return start, bias_ref[case]


def _softmax_rows(s, sink):
    m = jnp.max(s, axis=-1, keepdims=True)
    if sink is not None:
        m = jnp.maximum(m, sink)
    p = jnp.exp2(s - m)
    den = jnp.sum(p, axis=-1, keepdims=True)
    if sink is not None:
        den = den + jnp.exp2(sink - m)
    return p.astype(BF16), m, den


def _qk(qs, k2):
    return lax.dot_general(qs, k2, (((1,), (1,)), ((), ())), preferred_element_type=F32)


def _attn_a_kernel(sink_ref, q_ref, k_ref, v_ref, g_ref, bias_ref, o_ref,
                   *, tq, nsub, tk, seq, hw, offsets):
    for t in range(nsub):
        start, bias = _band_window(pl.program_id(1) * nsub + t, tq, tk, seq, hw, offsets, bias_ref)
        _attn_a_tile(sink_ref, q_ref, k_ref, v_ref, g_ref, o_ref, start, bias,
                     slice(t * tq, (t + 1) * tq), tq, tk)


def _attn_a_tile(sink_ref, q_ref, k_ref, v_ref, g_ref, o_ref, start, bias, rows, tq, tk):
    group = N_HEADS_A // N_KV_A
    k2 = k_ref[0, pl.ds(start, tk), :]
    v2 = v_ref[0, pl.ds(start, tk), :]
    v2r = _roll_heads(v2)
    lo = _lane_lo((tq, LANES))
    zero = jnp.zeros((tq, LANES), BF16)
    block = lax.broadcasted_iota(jnp.int32, (group * tq, 1), 0) // tq
    tiles = [None] * (N_HEADS_A // 2)
    for kv in range(N_KV_A):
        pa, pb = 2 * kv, 2 * kv + 1
        qa = q_ref[0, rows, pa * LANES:(pa + 1) * LANES]
        qb = q_ref[0, rows, pb * LANES:(pb + 1) * LANES]
        qar, qbr = _roll_heads(qa), _roll_heads(qb)
        if kv == 0:
            parts = [jnp.where(lo, qa, zero), jnp.where(lo, qb, zero),
                     jnp.where(lo, qar, zero), jnp.where(lo, qbr, zero)]
            v_even, v_odd = v2, v2r
        else:
            parts = [jnp.where(lo, zero, qar), jnp.where(lo, zero, qbr),
                     jnp.where(lo, zero, qa), jnp.where(lo, zero, qb)]
            v_even, v_odd = v2r, v2
        heads = (2 * pa, 2 * pb, 2 * pa + 1, 2 * pb + 1)
        sink = jnp.zeros((group * tq, 1), F32)
        for blk, head in enumerate(heads):
            sink = jnp.where(block == blk, sink_ref[head] * LOG2E, sink)
        p, _, den = _softmax_rows(_qk(jnp.concatenate(parts, axis=0), k2) + bias, sink)
        oe = jnp.dot(p[0:2 * tq], v_even, preferred_element_type=F32)
        oo = jnp.dot(p[2 * tq:], v_odd, preferred_element_type=F32)
        tiles[pa] = (jnp.where(lo, oe[0:tq], oo[0:tq])
                     * (1.0 / jnp.where(lo, den[0:tq], den[2 * tq:3 * tq])))
        tiles[pb] = (jnp.where(lo, oe[tq:], oo[tq:])
                     * (1.0 / jnp.where(lo, den[tq:2 * tq], den[3 * tq:])))
    sumsq = sum(jnp.sum(t * t, axis=-1, keepdims=True) for t in tiles)
    inv = lax.rsqrt(sumsq * (1.0 / D_A) + EPS)
    for c, t in enumerate(tiles):
        o_ref[0, rows, c * LANES:(c + 1) * LANES] = (
            t * inv * g_ref[:, c * LANES:(c + 1) * LANES]).astype(BF16)


def _attention_a(q, k, v, sink, g_oa, tq, nsub):
    n, s, _ = q.shape
    hw = WINDOW_A
    tk = min(tq + 2 * hw, s)
    rows = tq * nsub
    offsets, bias = _band_tables(tq, tk, s, hw, N_HEADS_A // N_KV_A)
    kern = functools.partial(_attn_a_kernel, tq=tq, nsub=nsub, tk=tk, seq=s, hw=hw,
                             offsets=offsets)
    return pl.pallas_call(
        kern,
        grid=(n, s // rows),
        in_specs=[
            pl.BlockSpec(memory_space=pltpu.SMEM),
            pl.BlockSpec((1, rows, D_A), lambda b, i: (b, i, 0)),
            pl.BlockSpec((1, s, D_KV_A), lambda b, i: (b, 0, 0)),
            pl.BlockSpec((1, s, D_KV_A), lambda b, i: (b, 0, 0)),
            pl.BlockSpec((1, D_A), lambda b, i: (0, 0)),
            pl.BlockSpec(bias.shape, lambda b, i: (0, 0, 0)),
        ],
        out_specs=pl.BlockSpec((1, rows, D_A), lambda b, i: (b, i, 0)),
        out_shape=jax.ShapeDtypeStruct((n, s, D_A), BF16),
        compiler_params=_params(2),
        name="attention_a",
    )(sink, q, k, v, g_oa, bias)


B_UNROLL = 16
B_TQ = 128
B_HW = 64
assert all(w // (2 * r) == B_HW for w, r in DILATION_PAIRS)
assert tuple(r for _, r in DILATION_PAIRS) == (1, 4, 16)


def _dilated_kernel(q1_ref, k1_ref, v1_ref, q4_ref, k4_ref, v4_ref, q16_ref, k16_ref, v16_ref,
                    bias1_ref, bias4_ref, bias16_ref, o_ref, acc_ref, m_ref, l_ref,
                    *, tn, seq, tables):
    tq = B_TQ
    step = pl.program_id(2)
    lo = _lane_lo((tq, LANES))
    zero = jnp.zeros((tq, LANES), BF16)

    def unit(refs, r, c, t):
        q_ref, k_ref, v_ref, bias_ref = refs
        sub = seq // r
        tk, offsets = tables[r]
        qi = step * (tn // r // tq) + t
        start, bias = _band_window(qi, tq, tk, sub, B_HW, offsets, bias_ref)
        q2 = q_ref[0, 0, c, pl.ds(pl.multiple_of(t * tq, tq), tq), :]
        k2 = k_ref[0, 0, c, pl.ds(start, tk), :]
        v2 = v_ref[0, 0, c, pl.ds(start, tk), :]
        qs = jnp.concatenate([jnp.where(lo, q2, zero), jnp.where(lo, zero, q2)], axis=0)
        p, m, den = _softmax_rows(_qk(qs, k2) + bias, None)
        o = jnp.dot(p, v2, preferred_element_type=F32)
        return (jnp.where(lo, o[0:tq], o[tq:]), jnp.where(lo, m[0:tq], m[tq:]),
                jnp.where(lo, den[0:tq], den[tq:]))

    def merge(rows, o, m, l):
        m_old = m_ref[rows, :]
        m_new = jnp.maximum(m_old, m)
        a = jnp.exp2(m_old - m_new)
        b = jnp.exp2(m - m_new)
        return acc_ref[rows, :] * a + o * b, m_new, l_ref[rows, :] * a + l * b

    refs16 = (q16_ref, k16_ref, v16_ref, bias16_ref)
    refs4 = (q4_ref, k4_ref, v4_ref, bias4_ref)
    refs1 = (q1_ref, k1_ref, v1_ref, bias1_ref)

    def first(grp, carry):
        for cc in range(B_UNROLL):
            c = grp * B_UNROLL + cc
            for t in range(tn // 16 // tq):
                rows = pl.ds(16 * t * tq + c, tq, stride=16)
                acc_ref[rows, :], m_ref[rows, :], l_ref[rows, :] = unit(refs16, 16, c, t)
        return carry
    lax.fori_loop(0, 16 // B_UNROLL, first, 0)

    def second(grp, carry):
        for tt in range(B_UNROLL // 4):
            t = grp * (B_UNROLL // 4) + tt
            for c in range(4):
                rows = pl.ds(4 * t * tq + c, tq, stride=4)
                acc_ref[rows, :], m_ref[rows, :], l_ref[rows, :] = merge(
                    rows, *unit(refs4, 4, c, t))
        return carry
    lax.fori_loop(0, tn // tq // B_UNROLL, second, 0)

    def third(grp, carry):
        for tt in range(B_UNROLL):
            t = grp * B_UNROLL + tt
            rows = pl.ds(pl.multiple_of(t * tq, tq), tq)
            acc, _, l = merge(rows, *unit(refs1, 1, 0, t))
            o_ref[0, 0, rows, :] = acc * (1.0 / l)
        return carry
    lax.fori_loop(0, tn // tq // B_UNROLL, third, 0)


def _dilated_mixture(b1, b4, b16, tn):
    _, n, _, seq, _ = b1.shape
    assert tn % (16 * B_TQ) == 0 and seq % tn == 0
    arrays = {1: b1, 4: b4, 16: b16}
    tables, biases, specs, operands = {}, {}, [], []
    for r in (1, 4, 16):
        sub = seq // r
        tk = min(B_TQ + 2 * B_HW, sub)
        offsets, biases[r] = _band_tables(B_TQ, tk, sub, B_HW, 2)
        tables[r] = (tk, offsets)
        specs.append(pl.BlockSpec((1, 1, r, tn // r, LANES), lambda b, p, i: (0, b, 0, i, p)))
        specs.append(pl.BlockSpec((1, 1, r, sub, LANES), lambda b, p, i: (1, b, 0, 0, p)))
        specs.append(pl.BlockSpec((1, 1, r, sub, LANES), lambda b, p, i: (2, b, 0, 0, p)))
        operands += [arrays[r]] * 3
    for r in (1, 4, 16):
        specs.append(pl.BlockSpec(biases[r].shape, lambda b, p, i: (0, 0, 0)))
        operands.append(biases[r])
    kern = functools.partial(_dilated_kernel, tn=tn, seq=seq, tables=tables)
    return pl.pallas_call(
        kern,
        grid=(n, N_SLABS, seq // tn),
        in_specs=specs,
        out_specs=pl.BlockSpec((1, 1, tn, LANES), lambda b, p, i: (b, p, i, 0)),
        out_shape=jax.ShapeDtypeStruct((n, N_SLABS, seq, LANES), F32),
        scratch_shapes=[pltpu.VMEM((tn, LANES), F32)] * 3,
        compiler_params=_params(3),
        name="dilated_mixture",
    )(*operands)


def _mem_kv_kernel(mem_ref, g_ref, w_ref, kv_ref):
    h = _rms(mem_ref[0], g_ref[...]).astype(BF16)
    kv_ref[0] = jnp.dot(h, w_ref[...], preferred_element_type=F32).astype(BF16)


def _mem_kv(mem, g_mem, w_ckv):
    n, m, _ = mem.shape
    return pl.pallas_call(
        _mem_kv_kernel,
        grid=(n,),
        in_specs=[pl.BlockSpec((1, m, D_MODEL), lambda b: (b, 0, 0)),
                  pl.BlockSpec((1, D_MODEL), lambda b: (0, 0)),
                  pl.BlockSpec((D_MODEL, 2 * D_X), lambda b: (0, 0))],
        out_specs=pl.BlockSpec((1, m, 2 * D_X), lambda b: (b, 0, 0)),
        out_shape=jax.ShapeDtypeStruct((n, m, 2 * D_X), BF16),
        compiler_params=_params(1),
        name="mem_kv",
    )(mem, g_mem, w_ckv)


def _mix_rows(rows, x_ref, oa_ref, ob_ref, g_ref, w_ref):
    tiles = [ob_ref[0, c, rows, :] for c in range(N_SLABS)]
    sumsq = sum(jnp.sum(t * t, axis=-1, keepdims=True) for t in tiles)
    rinv = lax.rsqrt(sumsq * (1.0 / D_B) + EPS)
    ob = jnp.concatenate(
        [(t * rinv * g_ref[:, c * LANES:(c + 1) * LANES]).astype(BF16)
         for c, t in enumerate(tiles)], axis=1)
    mix = jnp.concatenate([oa_ref[0, rows, :], ob], axis=1)
    return x_ref[0, rows, :] + jnp.dot(mix, w_ref[...], preferred_element_type=F32)


def _cross_rows(x, g_ref, wq_ref, kv_ref, wo_ref):
    h = _rms(x, g_ref[...]).astype(BF16)
    scale = HEAD_DIM_X ** -0.5 * LOG2E
    q = (jnp.dot(h, wq_ref[...], preferred_element_type=F32) * scale).astype(BF16)
    heads = []
    for hd in range(N_HEADS_X):
        cols = slice(hd * HEAD_DIM_X, (hd + 1) * HEAD_DIM_X)
        k = kv_ref[0, :, cols]
        v = kv_ref[0, :, D_X + hd * HEAD_DIM_X:D_X + (hd + 1) * HEAD_DIM_X]
        p, _, den = _softmax_rows(_qk(q[:, cols], k), None)
        heads.append((jnp.dot(p, v, preferred_element_type=F32) * (1.0 / den)).astype(BF16))
    return x + jnp.dot(jnp.concatenate(heads, axis=1), wo_ref[...], preferred_element_type=F32)


def _post_mixer_kernel(x_ref, oa_ref, ob_ref, gob_ref, wout_ref, gx_ref, wq_ref, kv_ref, wo_ref,
                       out_ref, *, ts, nsub):
    for t in range(nsub):
        rows = slice(t * ts, (t + 1) * ts)
        x1 = _mix_rows(rows, x_ref, oa_ref, ob_ref, gob_ref, wout_ref)
        out_ref[0, rows, :] = _cross_rows(x1, gx_ref, wq_ref, kv_ref, wo_ref)


def _post_mixer(x, oa, ob, g_ob, w_out, kv, g_cross, w_cq, w_co, ts, nsub):
    n, s, _ = x.shape
    m = kv.shape[1]
    tm = ts * nsub
    tok = lambda w: pl.BlockSpec((1, tm, w), lambda b, i: (b, i, 0))
    slab = pl.BlockSpec((1, N_SLABS, tm, LANES), lambda b, i: (b, 0, i, 0))
    const = lambda shape: pl.BlockSpec(shape, lambda b, i: (0,) * len(shape))
    return pl.pallas_call(
        functools.partial(_post_mixer_kernel, ts=ts, nsub=nsub),
        grid=(n, s // tm),
        in_specs=[tok(D_MODEL), tok(D_A), slab]
                 + [const((1, D_B)), const((D_MIX, D_MODEL)), const((1, D_MODEL)),
                    const((D_MODEL, D_X)),
                    pl.BlockSpec((1, m, 2 * D_X), lambda b, i: (b, 0, 0)),
                    const((D_X, D_MODEL))],
        out_specs=tok(D_MODEL),
        out_shape=jax.ShapeDtypeStruct((n, s, D_MODEL), F32),
        compiler_params=_params(2),
        name="post_mixer",
    )(x, oa, ob, g_ob, w_out, g_cross, w_cq, kv, w_co)


def _ffn_kernel(x_ref, xp_ref, xn_ref, g_ref, wgu_ref, cw_ref, cb_ref, wd_ref, gf_ref,
                out_ref, h_ref, act_ref, *, tm, fc, final_norm):
    i = pl.program_id(1)
    g = g_ref[...]
    h_ref[0:HALO_ROWS, :] = _rms(xp_ref[0], g).astype(BF16)
    h_ref[HALO_ROWS:HALO_ROWS + tm, :] = _rms(x_ref[0], g).astype(BF16)
    h_ref[HALO_ROWS + tm:, :] = _rms(xn_ref[0], g).astype(BF16)
    row = lax.broadcasted_iota(jnp.int32, (tm, fc), 0)
    last = lax.broadcasted_iota(jnp.int32, (HALO_ROWS, fc), 0) == HALO_ROWS - 1
    first_tile = i == 0
    last_tile = i == pl.num_programs(1) - 1
    for j in range(D_FF // fc):
        cols = slice(j * fc, (j + 1) * fc)
        gate = jnp.dot(h_ref[...], wgu_ref[:, cols], preferred_element_type=F32)
        up = jnp.dot(h_ref[HALO_ROWS:HALO_ROWS + tm, :], wgu_ref[:, D_FF + j * fc:D_FF + (j + 1) * fc],
                     preferred_element_type=F32)
        cur = gate[HALO_ROWS:HALO_ROWS + tm]
        prev_row = jnp.sum(jnp.where(last, gate[0:HALO_ROWS], 0.0), axis=0, keepdims=True)
        prev_row = jnp.where(first_tile, 0.0, prev_row)
        next_row = jnp.where(last_tile, 0.0, gate[HALO_ROWS + tm:HALO_ROWS + tm + 1])
        before = jnp.where(row == 0, prev_row, pltpu.roll(cur, 1, axis=0))
        after = jnp.where(row == tm - 1, next_row, pltpu.roll(cur, tm - 1, axis=0))
        conv = (before * cw_ref[0:1, cols] + cur * cw_ref[1:2, cols]
                + after * cw_ref[2:3, cols] + cb_ref[:, cols])
        act = 0.5 * conv * (1.0 + lax.erf(conv * (2.0 ** -0.5))) * up
        act_ref[:, cols] = act.astype(BF16)
    y = x_ref[0] + jnp.dot(act_ref[...], wd_ref[...], preferred_element_type=F32)
    if final_norm:
        y = _rms(y, gf_ref[...])
    out_ref[0] = y


def _conv_ffn(x, g_ffn, w_gu, conv_w, conv_b, w_down, g_final, final_norm, tm, fc):
    n, s, _ = x.shape
    hb = tm // HALO_ROWS
    last_hb = s // HALO_ROWS - 1
    tok = pl.BlockSpec((1, tm, D_MODEL), lambda b, i: (b, i, 0))
    prev = pl.BlockSpec((1, HALO_ROWS, D_MODEL), lambda b, i: (b, jnp.maximum(i * hb - 1, 0), 0))
    nxt = pl.BlockSpec((1, HALO_ROWS, D_MODEL),
                       lambda b, i: (b, jnp.minimum((i + 1) * hb, last_hb), 0))
    const = lambda shape: pl.BlockSpec(shape, lambda b, i: (0,) * len(shape))
    kern = functools.partial(_ffn_kernel, tm=tm, fc=fc, final_norm=final_norm)
    return pl.pallas_call(
        kern,
        grid=(n, s // tm),
        in_specs=[tok, prev, nxt, const((1, D_MODEL)), const((D_MODEL, 2 * D_FF)),
                  const((3, D_FF)), const((1, D_FF)), const((D_FF, D_MODEL)),
                  const((1, D_MODEL))],
        out_specs=tok,
        out_shape=jax.ShapeDtypeStruct((n, s, D_MODEL), F32),
        scratch_shapes=[pltpu.VMEM((tm + 2 * HALO_ROWS, D_MODEL), BF16),
                        pltpu.VMEM((tm, D_FF), BF16)],
        compiler_params=_params(2),
        name="conv_ffn",
    )(x, x, x, g_ffn, w_gu, conv_w, conv_b, w_down, g_final)


def _rope_tables(seq):
    inv = 1.0 / (ROPE_THETA ** (jnp.arange(0, HEAD_DIM, 2, dtype=F32) / HEAD_DIM))
    ang = jnp.arange(seq, dtype=F32)[:, None] * inv[None, :]
    cos, sin = jnp.cos(ang), jnp.sin(ang)
    reps = LANES // HEAD_DIM
    return (jnp.tile(jnp.concatenate([cos, cos], axis=-1), (1, reps)),
            jnp.tile(jnp.concatenate([-sin, sin], axis=-1), (1, reps)))


def _trunk(x, mem, p):
    n, s, _ = x.shape
    cos, sin = _rope_tables(s)
    row = lambda v: v.reshape(1, -1)
    for l in range(DEPTH):
        qa, ka, va, b1, b4, b16 = _in_projection(
            x, row(p['g_mix'][l]), p['w_in'][l], cos, sin, tm=512)
        oa = _attention_a(qa, ka, va, p['sink_a'][l], row(p['g_out_a'][l]), tq=128, nsub=4)
        ob = _dilated_mixture(b1.reshape(3, n, 1, s, D_B), b4, b16, tn=2048)
        kv = _mem_kv(mem, row(p['g_mem'][l]), p['w_ckv'][l])
        x = _post_mixer(x, oa, ob, row(p['g_out_b'][l]), p['w_out'][l], kv,
                        row(p['g_cross'][l]), p['w_cq'][l], p['w_co'][l], ts=256, nsub=2)
        x = _conv_ffn(x, row(p['g_ffn'][l]), p['w_gu'][l], p['conv_w'][l],
                      row(p['conv_b'][l]), p['w_down'][l], row(p['g_final']),
                      final_norm=(l == DEPTH - 1), tm=512, fc=256)
    return x


def kernel(x_prompt, x_sample, mem_prompt, mem_sample, g_mix, w_in, sink_a, g_out_a, g_out_b, w_out, g_cross, g_mem, w_cq, w_ckv, w_co, g_ffn, w_gu, conv_w, conv_b, w_down, g_final):
    p = dict(g_mix=g_mix, sink_a=sink_a, g_out_a=g_out_a, g_out_b=g_out_b, g_cross=g_cross,
             g_mem=g_mem, g_ffn=g_ffn, conv_w=conv_w, conv_b=conv_b, g_final=g_final,
             w_in=w_in.astype(BF16), w_out=w_out.astype(BF16), w_cq=w_cq.astype(BF16),
             w_ckv=w_ckv.astype(BF16), w_co=w_co.astype(BF16), w_gu=w_gu.astype(BF16),
             w_down=w_down.astype(BF16))
    return (_trunk(x_prompt, mem_prompt, p), _trunk(x_sample, mem_sample, p))
```

```python
import functools
import math

import numpy as np
import jax
import jax.numpy as jnp
from jax import lax
from jax.experimental import pallas as pl
from jax.experimental.pallas import tpu as pltpu

D_MODEL = 1024
DEPTH = 2
HEAD_DIM = 64
N_HEADS_A = 8
N_KV_A = 2
N_HEADS_B = 8
D_A = N_HEADS_A * HEAD_DIM
D_KV_A = N_KV_A * HEAD_DIM
D_B = N_HEADS_B * HEAD_DIM
D_MIX = D_A + D_B
D_IN = D_A + 2 * D_KV_A + 3 * D_B
WINDOW_A = 128
DILATION_PAIRS = ((128, 1), (512, 4), (2048, 16))
N_HEADS_X = 4
HEAD_DIM_X = 128
D_X = N_HEADS_X * HEAD_DIM_X
D_FF = 2816
ROPE_THETA = 10000.0
EPS = 1e-6
NEG = -1e30
LOG2E = math.log2(math.e)

LANES = 128
HALO_ROWS = 16
N_SLABS = D_B // LANES
VMEM_LIMIT = 56 * 1024 * 1024

F32 = jnp.float32
BF16 = jnp.bfloat16


def _params(n_axes):
    return pltpu.CompilerParams(
        dimension_semantics=("arbitrary",) * n_axes, vmem_limit_bytes=VMEM_LIMIT)


def _rms(xf, g):
    return xf * lax.rsqrt(jnp.mean(xf * xf, axis=-1, keepdims=True) + EPS) * g


def _lane_lo(shape):
    return lax.broadcasted_iota(jnp.int32, shape, 1) % LANES < HEAD_DIM


def _roll_heads(t):
    return pltpu.roll(t.astype(F32), HEAD_DIM, axis=1).astype(t.dtype)


def _proj_kernel(x_ref, g_ref, w_ref, cos_ref, sin_ref,
                 qa_ref, ka_ref, va_ref, b1_ref, b4_ref, b16_ref, nat_ref, sub_ref, *, tm):
    h = _rms(x_ref[0], g_ref[...]).astype(BF16)
    cos = cos_ref[...]
    sin = sin_ref[...]
    first_half = lax.broadcasted_iota(jnp.int32, cos.shape, 1) % HEAD_DIM < HEAD_DIM // 2

    def rope(y, scale):
        partner = jnp.where(first_half,
                            pltpu.roll(y, LANES - HEAD_DIM // 2, axis=1),
                            pltpu.roll(y, HEAD_DIM // 2, axis=1))
        out = y * cos + partner * sin
        return out * scale if scale != 1.0 else out

    def project(col, width):
        return jnp.dot(h, w_ref[:, col:col + width], preferred_element_type=F32)

    q_scale = HEAD_DIM ** -0.5 * LOG2E
    ya = project(0, D_A)
    for c in range(D_A // LANES):
        cols = slice(c * LANES, (c + 1) * LANES)
        qa_ref[0, :, cols] = rope(ya[:, cols], q_scale).astype(BF16)
    ykv = project(D_A, 2 * D_KV_A)
    ka_ref[0] = rope(ykv[:, 0:D_KV_A], 1.0).astype(BF16)
    va_ref[0] = ykv[:, D_KV_A:].astype(BF16)

    off = D_A + 2 * D_KV_A
    for part, (roped, scale) in enumerate(((True, q_scale), (True, 1.0), (False, 1.0))):
        yb = project(off + part * D_B, D_B)
        for c in range(N_SLABS):
            cols = slice(c * LANES, (c + 1) * LANES)
            y = rope(yb[:, cols], scale) if roped else yb[:, cols]
            b1_ref[part, 0, :, cols] = y.astype(BF16)
            nat_ref[...] = y
            for c0 in range(4):
                t4 = nat_ref[pl.ds(c0, tm // 4, stride=4), :]
                b4_ref[part, 0, c0, :, cols] = t4.astype(BF16)
                sub_ref[c0] = t4
            for c0 in range(4):
                for c1 in range(4):
                    t16 = sub_ref[c0, pl.ds(c1, tm // 16, stride=4), :]
                    b16_ref[part, 0, c0 + 4 * c1, :, cols] = t16.astype(BF16)


def _in_projection(x, g, w_in, cos, sin, tm):
    n, s, _ = x.shape
    tok = lambda w: pl.BlockSpec((1, tm, w), lambda i, b: (b, i, 0))
    const = lambda shape: pl.BlockSpec(shape, lambda i, b: (0,) * len(shape))
    tab = pl.BlockSpec((tm, LANES), lambda i, b: (i, 0))
    sub = lambda r: pl.BlockSpec((3, 1, r, tm // r, D_B), lambda i, b: (0, b, 0, i, 0))
    return pl.pallas_call(
        functools.partial(_proj_kernel, tm=tm),
        grid=(s // tm, n),
        in_specs=[tok(D_MODEL), const((1, D_MODEL)), const((D_MODEL, D_IN)), tab, tab],
        out_specs=[tok(D_A), tok(D_KV_A), tok(D_KV_A),
                   pl.BlockSpec((3, 1, tm, D_B), lambda i, b: (0, b, i, 0)), sub(4), sub(16)],
        out_shape=[jax.ShapeDtypeStruct((n, s, D_A), BF16),
                   jax.ShapeDtypeStruct((n, s, D_KV_A), BF16),
                   jax.ShapeDtypeStruct((n, s, D_KV_A), BF16),
                   jax.ShapeDtypeStruct((3, n, s, D_B), BF16),
                   jax.ShapeDtypeStruct((3, n, 4, s // 4, D_B), BF16),
                   jax.ShapeDtypeStruct((3, n, 16, s // 16, D_B), BF16)],
        scratch_shapes=[pltpu.VMEM((tm, LANES), F32), pltpu.VMEM((4, tm // 4, LANES), F32)],
        compiler_params=_params(2),
        name="in_projection",
    )(x, g, w_in, cos, sin)


def _window_start(qi, tq, tk, seq, hw):
    return min(max(qi * tq - hw, 0), seq - tk)


def _band_tables(tq, tk, seq, hw, stack):
    offsets = sorted({qi * tq - _window_start(qi, tq, tk, seq, hw) for qi in range(seq // tq)})
    row = np.arange(stack * tq)[:, None] % tq
    col = np.arange(tk)[None, :]
    tabs = [np.where(np.abs(row + off - col) <= hw, 0.0, NEG) for off in offsets]
    return tuple(offsets), jnp.asarray(np.stack(tabs), F32)


def _band_window(qi, tq, tk, seq, hw, offsets, bias_ref):
    q0 = qi * tq
    start = pl.multiple_of(jnp.clip(q0 - hw, 0, seq - tk), hw)
    case = 0
    for idx, off in enumerate(offsets):
        case = jnp.where(q0 - start == off, idx, case)
    return start, bias_ref[case]


def _softmax_rows(s, sink):
    m = jnp.max(s, axis=-1, keepdims=True)
    if sink is not None:
        m = jnp.maximum(m, sink)
    p = jnp.exp2(s - m)
    den = jnp.sum(p, axis=-1, keepdims=True)
    if sink is not None:
        den = den + jnp.exp2(sink - m)
    return p.astype(BF16), m, den


def _qk(qs, k2):
    return lax.dot_general(qs, k2, (((1,), (1,)), ((), ())), preferred_element_type=F32)


def _attn_a_kernel(sink_ref, q_ref, k_ref, v_ref, g_ref, bias_ref, o_ref,
                   *, tq, nsub, tk, seq, hw, offsets):
    for t in range(nsub):
        start, bias = _band_window(pl.program_id(1) * nsub + t, tq, tk, seq, hw, offsets, bias_ref)
        _attn_a_tile(sink_ref, q_ref, k_ref, v_ref, g_ref, o_ref, start, bias,
                     slice(t * tq, (t + 1) * tq), tq, tk)


def _attn_a_tile(sink_ref, q_ref, k_ref, v_ref, g_ref, o_ref, start, bias, rows, tq, tk):
    group = N_HEADS_A // N_KV_A
    k2 = k_ref[0, pl.ds(start, tk), :]
    v2 = v_ref[0, pl.ds(start, tk), :]
    v2r = _roll_heads(v2)
    lo = _lane_lo((tq, LANES))
    zero = jnp.zeros((tq, LANES), BF16)
    block = lax.broadcasted_iota(jnp.int32, (group * tq, 1), 0) // tq
    tiles = [None] * (N_HEADS_A // 2)
    for kv in range(N_KV_A):
        pa, pb = 2 * kv, 2 * kv + 1
        qa = q_ref[0, rows, pa * LANES:(pa + 1) * LANES]
        qb = q_ref[0, rows, pb * LANES:(pb + 1) * LANES]
        qar, qbr = _roll_heads(qa), _roll_heads(qb)
        if kv == 0:
            parts = [jnp.where(lo, qa, zero), jnp.where(lo, qb, zero),
                     jnp.where(lo, qar, zero), jnp.where(lo, qbr, zero)]
            v_even, v_odd = v2, v2r
        else:
            parts = [jnp.where(lo, zero, qar), jnp.where(lo, zero, qbr),
                     jnp.where(lo, zero, qa), jnp.where(lo, zero, qb)]
            v_even, v_odd = v2r, v2
        heads = (2 * pa, 2 * pb, 2 * pa + 1, 2 * pb + 1)
        sink = jnp.zeros((group * tq, 1), F32)
        for blk, head in enumerate(heads):
            sink = jnp.where(block == blk, sink_ref[head] * LOG2E, sink)
        p, _, den = _softmax_rows(_qk(jnp.concatenate(parts, axis=0), k2) + bias, sink)
        oe = jnp.dot(p[0:2 * tq], v_even, preferred_element_type=F32)
        oo = jnp.dot(p[2 * tq:], v_odd, preferred_element_type=F32)
        tiles[pa] = (jnp.where(lo, oe[0:tq], oo[0:tq])
                     * (1.0 / jnp.where(lo, den[0:tq], den[2 * tq:3 * tq])))
        tiles[pb] = (jnp.where(lo, oe[tq:], oo[tq:])
                     * (1.0 / jnp.where(lo, den[tq:2 * tq], den[3 * tq:])))
    sumsq = sum(jnp.sum(t * t, axis=-1, keepdims=True) for t in tiles)
    inv = lax.rsqrt(sumsq * (1.0 / D_A) + EPS)
    for c, t in enumerate(tiles):
        o_ref[0, rows, c * LANES:(c + 1) * LANES] = (
            t * inv * g_ref[:, c * LANES:(c + 1) * LANES]).astype(BF16)


def _attention_a(q, k, v, sink, g_oa, tq, nsub):
    n, s, _ = q.shape
    hw = WINDOW_A
    tk = min(tq + 2 * hw, s)
    rows = tq * nsub
    offsets, bias = _band_tables(tq, tk, s, hw, N_HEADS_A // N_KV_A)
    kern = functools.partial(_attn_a_kernel, tq=tq, nsub=nsub, tk=tk, seq=s, hw=hw,
                             offsets=offsets)
    return pl.pallas_call(
        kern,
        grid=(n, s // rows),
        in_specs=[
            pl.BlockSpec(memory_space=pltpu.SMEM),
            pl.BlockSpec((1, rows, D_A), lambda b, i: (b, i, 0)),
            pl.BlockSpec((1, s, D_KV_A), lambda b, i: (b, 0, 0)),
            pl.BlockSpec((1, s, D_KV_A), lambda b, i: (b, 0, 0)),
            pl.BlockSpec((1, D_A), lambda b, i: (0, 0)),
            pl.BlockSpec(bias.shape, lambda b, i: (0, 0, 0)),
        ],
        out_specs=pl.BlockSpec((1, rows, D_A), lambda b, i: (b, i, 0)),
        out_shape=jax.ShapeDtypeStruct((n, s, D_A), BF16),
        compiler_params=_params(2),
        name="attention_a",
    )(sink, q, k, v, g_oa, bias)


B_UNROLL = 16
B_TQ = 128
B_HW = 64
assert all(w // (2 * r) == B_HW for w, r in DILATION_PAIRS)
assert tuple(r for _, r in DILATION_PAIRS) == (1, 4, 16)


def _dilated_kernel(q1_ref, k1_ref, v1_ref, q4_ref, k4_ref, v4_ref, q16_ref, k16_ref, v16_ref,
                    bias1_ref, bias4_ref, bias16_ref, o_ref, acc_ref, m_ref, l_ref,
                    *, tn, seq, tables):
    tq = B_TQ
    step = pl.program_id(2)
    lo = _lane_lo((tq, LANES))
    zero = jnp.zeros((tq, LANES), BF16)

    def unit(refs, r, c, t):
        q_ref, k_ref, v_ref, bias_ref = refs
        sub = seq // r
        tk, offsets = tables[r]
        qi = step * (tn // r // tq) + t
        start, bias = _band_window(qi, tq, tk, sub, B_HW, offsets, bias_ref)
        q2 = q_ref[0, 0, c, pl.ds(pl.multiple_of(t * tq, tq), tq), :]
        k2 = k_ref[0, 0, c, pl.ds(start, tk), :]
        v2 = v_ref[0, 0, c, pl.ds(start, tk), :]
        qs = jnp.concatenate([jnp.where(lo, q2, zero), jnp.where(lo, zero, q2)], axis=0)
        p, m, den = _softmax_rows(_qk(qs, k2) + bias, None)
        o = jnp.dot(p, v2, preferred_element_type=F32)
        return (jnp.where(lo, o[0:tq], o[tq:]), jnp.where(lo, m[0:tq], m[tq:]),
                jnp.where(lo, den[0:tq], den[tq:]))

    def merge(rows, o, m, l):
        m_old = m_ref[rows, :]
        m_new = jnp.maximum(m_old, m)
        a = jnp.exp2(m_old - m_new)
        b = jnp.exp2(m - m_new)
        return acc_ref[rows, :] * a + o * b, m_new, l_ref[rows, :] * a + l * b

    refs16 = (q16_ref, k16_ref, v16_ref, bias16_ref)
    refs4 = (q4_ref, k4_ref, v4_ref, bias4_ref)
    refs1 = (q1_ref, k1_ref, v1_ref, bias1_ref)

    def first(grp, carry):
        for cc in range(B_UNROLL):
            c = grp * B_UNROLL + cc
            for t in range(tn // 16 // tq):
                rows = pl.ds(16 * t * tq + c, tq, stride=16)
                acc_ref[rows, :], m_ref[rows, :], l_ref[rows, :] = unit(refs16, 16, c, t)
        return carry
    lax.fori_loop(0, 16 // B_UNROLL, first, 0)

    def second(grp, carry):
        for tt in range(B_UNROLL // 4):
            t = grp * (B_UNROLL // 4) + tt
            for c in range(4):
                rows = pl.ds(4 * t * tq + c, tq, stride=4)
                acc_ref[rows, :], m_ref[rows, :], l_ref[rows, :] = merge(
                    rows, *unit(refs4, 4, c, t))
        return carry
    lax.fori_loop(0, tn // tq // B_UNROLL, second, 0)

    def third(grp, carry):
        for tt in range(B_UNROLL):
            t = grp * B_UNROLL + tt
            rows = pl.ds(pl.multiple_of(t * tq, tq), tq)
            acc, _, l = merge(rows, *unit(refs1, 1, 0, t))
            o_ref[0, 0, rows, :] = acc * (1.0 / l)
        return carry
    lax.fori_loop(0, tn // tq // B_UNROLL, third, 0)


def _dilated_mixture(b1, b4, b16, tn):
    _, n, _, seq, _ = b1.shape
    assert tn % (16 * B_TQ) == 0 and seq % tn == 0
    arrays = {1: b1, 4: b4, 16: b16}
    tables, biases, specs, operands = {}, {}, [], []
    for r in (1, 4, 16):
        sub = seq // r
        tk = min(B_TQ + 2 * B_HW, sub)
        offsets, biases[r] = _band_tables(B_TQ, tk, sub, B_HW, 2)
        tables[r] = (tk, offsets)
        specs.append(pl.BlockSpec((1, 1, r, tn // r, LANES), lambda b, p, i: (0, b, 0, i, p)))
        specs.append(pl.BlockSpec((1, 1, r, sub, LANES), lambda b, p, i: (1, b, 0, 0, p)))
        specs.append(pl.BlockSpec((1, 1, r, sub, LANES), lambda b, p, i: (2, b, 0, 0, p)))
        operands += [arrays[r]] * 3
    for r in (1, 4, 16):
        specs.append(pl.BlockSpec(biases[r].shape, lambda b, p, i: (0, 0, 0)))
        operands.append(biases[r])
    kern = functools.partial(_dilated_kernel, tn=tn, seq=seq, tables=tables)
    return pl.pallas_call(
        kern,
        grid=(n, N_SLABS, seq // tn),
        in_specs=specs,
        out_specs=pl.BlockSpec((1, 1, tn, LANES), lambda b, p, i: (b, p, i, 0)),
        out_shape=jax.ShapeDtypeStruct((n, N_SLABS, seq, LANES), F32),
        scratch_shapes=[pltpu.VMEM((tn, LANES), F32)] * 3,
        compiler_params=_params(3),
        name="dilated_mixture",
    )(*operands)


def _mem_kv_kernel(mem_ref, g_ref, w_ref, kv_ref):
    h = _rms(mem_ref[0], g_ref[...]).astype(BF16)
    kv_ref[0] = jnp.dot(h, w_ref[...], preferred_element_type=F32).astype(BF16)


def _mem_kv(mem, g_mem, w_ckv):
    n, m, _ = mem.shape
    return pl.pallas_call(
        _mem_kv_kernel,
        grid=(n,),
        in_specs=[pl.BlockSpec((1, m, D_MODEL), lambda b: (b, 0, 0)),
                  pl.BlockSpec((1, D_MODEL), lambda b: (0, 0)),
                  pl.BlockSpec((D_MODEL, 2 * D_X), lambda b: (0, 0))],
        out_specs=pl.BlockSpec((1, m, 2 * D_X), lambda b: (b, 0, 0)),
        out_shape=jax.ShapeDtypeStruct((n, m, 2 * D_X), BF16),
        compiler_params=_params(1),
        name="mem_kv",
    )(mem, g_mem, w_ckv)


def _mix_rows(rows, x_ref, oa_ref, ob_ref, g_ref, w_ref):
    tiles = [ob_ref[0, c, rows, :] for c in range(N_SLABS)]
    sumsq = sum(jnp.sum(t * t, axis=-1, keepdims=True) for t in tiles)
    rinv = lax.rsqrt(sumsq * (1.0 / D_B) + EPS)
    ob = jnp.concatenate(
        [(t * rinv * g_ref[:, c * LANES:(c + 1) * LANES]).astype(BF16)
         for c, t in enumerate(tiles)], axis=1)
    mix = jnp.concatenate([oa_ref[0, rows, :], ob], axis=1)
    return x_ref[0, rows, :] + jnp.dot(mix, w_ref[...], preferred_element_type=F32)


def _cross_rows(x, g_ref, wq_ref, kv_ref, wo_ref):
    h = _rms(x, g_ref[...]).astype(BF16)
    scale = HEAD_DIM_X ** -0.5 * LOG2E
    q = (jnp.dot(h, wq_ref[...], preferred_element_type=F32) * scale).astype(BF16)
    heads = []
    for hd in range(N_HEADS_X):
        cols = slice(hd * HEAD_DIM_X, (hd + 1) * HEAD_DIM_X)
        k = kv_ref[0, :, cols]
        v = kv_ref[0, :, D_X + hd * HEAD_DIM_X:D_X + (hd + 1) * HEAD_DIM_X]
        p, _, den = _softmax_rows(_qk(q[:, cols], k), None)
        heads.append((jnp.dot(p, v, preferred_element_type=F32) * (1.0 / den)).astype(BF16))
    return x + jnp.dot(jnp.concatenate(heads, axis=1), wo_ref[...], preferred_element_type=F32)


def _post_mixer_kernel(x_ref, oa_ref, ob_ref, gob_ref, wout_ref, gx_ref, wq_ref, kv_ref, wo_ref,
                       out_ref, *, ts, nsub):
    for t in range(nsub):
        rows = slice(t * ts, (t + 1) * ts)
        x1 = _mix_rows(rows, x_ref, oa_ref, ob_ref, gob_ref, wout_ref)
        out_ref[0, rows, :] = _cross_rows(x1, gx_ref, wq_ref, kv_ref, wo_ref)


def _post_mixer(x, oa, ob, g_ob, w_out, kv, g_cross, w_cq, w_co, ts, nsub):
    n, s, _ = x.shape
    m = kv.shape[1]
    tm = ts * nsub
    tok = lambda w: pl.BlockSpec((1, tm, w), lambda b, i: (b, i, 0))
    slab = pl.BlockSpec((1, N_SLABS, tm, LANES), lambda b, i: (b, 0, i, 0))
    const = lambda shape: pl.BlockSpec(shape, lambda b, i: (0,) * len(shape))
    return pl.pallas_call(
        functools.partial(_post_mixer_kernel, ts=ts, nsub=nsub),
        grid=(n, s // tm),
        in_specs=[tok(D_MODEL), tok(D_A), slab]
                 + [const((1, D_B)), const((D_MIX, D_MODEL)), const((1, D_MODEL)),
                    const((D_MODEL, D_X)),
                    pl.BlockSpec((1, m, 2 * D_X), lambda b, i: (b, 0, 0)),
                    const((D_X, D_MODEL))],
        out_specs=tok(D_MODEL),
        out_shape=jax.ShapeDtypeStruct((n, s, D_MODEL), F32),
        compiler_params=_params(2),
        name="post_mixer",
    )(x, oa, ob, g_ob, w_out, g_cross, w_cq, kv, w_co)


def _ffn_kernel(x_ref, xp_ref, xn_ref, g_ref, wgu_ref, cw_ref, cb_ref, wd_ref, gf_ref,
                out_ref, h_ref, act_ref, *, tm, fc, final_norm):
    i = pl.program_id(1)
    g = g_ref[...]
    h_ref[0:HALO_ROWS, :] = _rms(xp_ref[0], g).astype(BF16)
    h_ref[HALO_ROWS:HALO_ROWS + tm, :] = _rms(x_ref[0], g).astype(BF16)
    h_ref[HALO_ROWS + tm:, :] = _rms(xn_ref[0], g).astype(BF16)
    row = lax.broadcasted_iota(jnp.int32, (tm, fc), 0)
    last = lax.broadcasted_iota(jnp.int32, (HALO_ROWS, fc), 0) == HALO_ROWS - 1
    first_tile = i == 0
    last_tile = i == pl.num_programs(1) - 1
    for j in range(D_FF // fc):
        cols = slice(j * fc, (j + 1) * fc)
        gate = jnp.dot(h_ref[...], wgu_ref[:, cols], preferred_element_type=F32)
        up = jnp.dot(h_ref[HALO_ROWS:HALO_ROWS + tm, :], wgu_ref[:, D_FF + j * fc:D_FF + (j + 1) * fc],
                     preferred_element_type=F32)
        cur = gate[HALO_ROWS:HALO_ROWS + tm]
        prev_row = jnp.sum(jnp.where(last, gate[0:HALO_ROWS], 0.0), axis=0, keepdims=True)
        prev_row = jnp.where(first_tile, 0.0, prev_row)
        next_row = jnp.where(last_tile, 0.0, gate[HALO_ROWS + tm:HALO_ROWS + tm + 1])
        before = jnp.where(row == 0, prev_row, pltpu.roll(cur, 1, axis=0))
        after = jnp.where(row == tm - 1, next_row, pltpu.roll(cur, tm - 1, axis=0))
        conv = (before * cw_ref[0:1, cols] + cur * cw_ref[1:2, cols]
                + after * cw_ref[2:3, cols] + cb_ref[:, cols])
        act = 0.5 * conv * (1.0 + lax.erf(conv * (2.0 ** -0.5))) * up
        act_ref[:, cols] = act.astype(BF16)
    y = x_ref[0] + jnp.dot(act_ref[...], wd_ref[...], preferred_element_type=F32)
    if final_norm:
        y = _rms(y, gf_ref[...])
    out_ref[0] = y


def _conv_ffn(x, g_ffn, w_gu, conv_w, conv_b, w_down, g_final, final_norm, tm, fc):
    n, s, _ = x.shape
    hb = tm // HALO_ROWS
    last_hb = s // HALO_ROWS - 1
    tok = pl.BlockSpec((1, tm, D_MODEL), lambda b, i: (b, i, 0))
    prev = pl.BlockSpec((1, HALO_ROWS, D_MODEL), lambda b, i: (b, jnp.maximum(i * hb - 1, 0), 0))
    nxt = pl.BlockSpec((1, HALO_ROWS, D_MODEL),
                       lambda b, i: (b, jnp.minimum((i + 1) * hb, last_hb), 0))
    const = lambda shape: pl.BlockSpec(shape, lambda b, i: (0,) * len(shape))
    kern = functools.partial(_ffn_kernel, tm=tm, fc=fc, final_norm=final_norm)
    return pl.pallas_call(
        kern,
        grid=(n, s // tm),
        in_specs=[tok, prev, nxt, const((1, D_MODEL)), const((D_MODEL, 2 * D_FF)),
                  const((3, D_FF)), const((1, D_FF)), const((D_FF, D_MODEL)),
                  const((1, D_MODEL))],
        out_specs=tok,
        out_shape=jax.ShapeDtypeStruct((n, s, D_MODEL), F32),
        scratch_shapes=[pltpu.VMEM((tm + 2 * HALO_ROWS, D_MODEL), BF16),
                        pltpu.VMEM((tm, D_FF), BF16)],
        compiler_params=_params(2),
        name="conv_ffn",
    )(x, x, x, g_ffn, w_gu, conv_w, conv_b, w_down, g_final)


def _rope_tables(seq):
    inv = 1.0 / (ROPE_THETA ** (jnp.arange(0, HEAD_DIM, 2, dtype=F32) / HEAD_DIM))
    ang = jnp.arange(seq, dtype=F32)[:, None] * inv[None, :]
    cos, sin = jnp.cos(ang), jnp.sin(ang)
    reps = LANES // HEAD_DIM
    return (jnp.tile(jnp.concatenate([cos, cos], axis=-1), (1, reps)),
            jnp.tile(jnp.concatenate([-sin, sin], axis=-1), (1, reps)))


def _trunk(x, mem, p):
    n, s, _ = x.shape
    cos, sin = _rope_tables(s)
    row = lambda v: v.reshape(1, -1)
    for l in range(DEPTH):
        qa, ka, va, b1, b4, b16 = _in_projection(
            x, row(p['g_mix'][l]), p['w_in'][l], cos, sin, tm=1024)
        oa = _attention_a(qa, ka, va, p['sink_a'][l], row(p['g_out_a'][l]), tq=128, nsub=8)
        ob = _dilated_mixture(b1.reshape(3, n, 1, s, D_B), b4, b16, tn=2048)
        kv = _mem_kv(mem, row(p['g_mem'][l]), p['w_ckv'][l])
        x = _post_mixer(x, oa, ob, row(p['g_out_b'][l]), p['w_out'][l], kv,
                        row(p['g_cross'][l]), p['w_cq'][l], p['w_co'][l], ts=512, nsub=2)
        x = _conv_ffn(x, row(p['g_ffn'][l]), p['w_gu'][l], p['conv_w'][l],
                      row(p['conv_b'][l]), p['w_down'][l], row(p['g_final']),
                      final_norm=(l == DEPTH - 1), tm=1024, fc=256)
    return x


def kernel(x_prompt, x_sample, mem_prompt, mem_sample, g_mix, w_in, sink_a, g_out_a, g_out_b, w_out, g_cross, g_mem, w_cq, w_ckv, w_co, g_ffn, w_gu, conv_w, conv_b, w_down, g_final):
    p = dict(g_mix=g_mix, sink_a=sink_a, g_out_a=g_out_a, g_out_b=g_out_b, g_cross=g_cross,
             g_mem=g_mem, g_ffn=g_ffn, conv_w=conv_w, conv_b=conv_b, g_final=g_final)
    weights = dict(w_in=w_in, w_out=w_out, w_cq=w_cq, w_ckv=w_ckv, w_co=w_co, w_gu=w_gu,
                   w_down=w_down)
    for name, w in weights.items():
        p[name] = [w[l].astype(BF16) for l in range(DEPTH)]
    return (_trunk(x_prompt, mem_prompt, p), _trunk(x_sample, mem_sample, p))
```

```python
import functools
import math

import numpy as np
import jax
import jax.numpy as jnp
from jax import lax
from jax.experimental import pallas as pl
from jax.experimental.pallas import tpu as pltpu

D_MODEL = 1024
DEPTH = 2
HEAD_DIM = 64
N_HEADS_A = 8
N_KV_A = 2
N_HEADS_B = 8
D_A = N_HEADS_A * HEAD_DIM
D_KV_A = N_KV_A * HEAD_DIM
D_B = N_HEADS_B * HEAD_DIM
D_MIX = D_A + D_B
D_IN = D_A + 2 * D_KV_A + 3 * D_B
WINDOW_A = 128
DILATION_PAIRS = ((128, 1), (512, 4), (2048, 16))
N_HEADS_X = 4
HEAD_DIM_X = 128
D_X = N_HEADS_X * HEAD_DIM_X
D_FF = 2816
ROPE_THETA = 10000.0
EPS = 1e-6
NEG = -1e30
LOG2E = math.log2(math.e)

LANES = 128
HALO_ROWS = 16
N_SLABS = D_B // LANES
VMEM_LIMIT = 56 * 1024 * 1024

F32 = jnp.float32
BF16 = jnp.bfloat16


def _params(n_axes):
    return pltpu.CompilerParams(
        dimension_semantics=("arbitrary",) * n_axes, vmem_limit_bytes=VMEM_LIMIT)


def _layer_weight(shape, layer):
    return pl.BlockSpec((None,) + tuple(shape), lambda *_: (layer,) + (0,) * len(shape),
                        pipeline_mode=pl.Buffered(1))


def _rms(xf, g):
    return xf * lax.rsqrt(jnp.mean(xf * xf, axis=-1, keepdims=True) + EPS) * g


def _lane_lo(shape):
    return lax.broadcasted_iota(jnp.int32, shape, 1) % LANES < HEAD_DIM


def _roll_heads(t):
    return pltpu.roll(t.astype(F32), HEAD_DIM, axis=1).astype(t.dtype)


def _proj_kernel(x_ref, g_ref, w_ref, cos_ref, sin_ref,
                 qa_ref, ka_ref, va_ref, b1_ref, b4_ref, b16_ref, nat_ref, sub_ref, *, tm):
    h = _rms(x_ref[0], g_ref[...]).astype(BF16)
    cos = cos_ref[...]
    sin = sin_ref[...]
    first_half = lax.broadcasted_iota(jnp.int32, cos.shape, 1) % HEAD_DIM < HEAD_DIM // 2

    def rope(y, scale):
        partner = jnp.where(first_half,
                            pltpu.roll(y, LANES - HEAD_DIM // 2, axis=1),
                            pltpu.roll(y, HEAD_DIM // 2, axis=1))
        out = y * cos + partner * sin
        return out * scale if scale != 1.0 else out

    def project(col, width):
        return jnp.dot(h, w_ref[:, col:col + width], preferred_element_type=F32)

    q_scale = HEAD_DIM ** -0.5 * LOG2E
    ya = project(0, D_A)
    for c in range(D_A // LANES):
        cols = slice(c * LANES, (c + 1) * LANES)
        qa_ref[0, :, cols] = rope(ya[:, cols], q_scale).astype(BF16)
    ykv = project(D_A, 2 * D_KV_A)
    ka_ref[0] = rope(ykv[:, 0:D_KV_A], 1.0).astype(BF16)
    va_ref[0] = ykv[:, D_KV_A:].astype(BF16)

    off = D_A + 2 * D_KV_A
    for part, (roped, scale) in enumerate(((True, q_scale), (True, 1.0), (False, 1.0))):
        yb = project(off + part * D_B, D_B)
        for c in range(N_SLABS):
            cols = slice(c * LANES, (c + 1) * LANES)
            y = rope(yb[:, cols], scale) if roped else yb[:, cols]
            b1_ref[part, 0, :, cols] = y.astype(BF16)
            nat_ref[c] = y
            for c0 in range(4):
                t4 = nat_ref[c, pl.ds(c0, tm // 4, stride=4), :]
                b4_ref[part, 0, c0, :, cols] = t4.astype(BF16)
                sub_ref[c, c0] = t4
            for c0 in range(4):
                for c1 in range(4):
                    t16 = sub_ref[c, c0, pl.ds(c1, tm // 16, stride=4), :]
                    b16_ref[part, 0, c0 + 4 * c1, :, cols] = t16.astype(BF16)


def _in_projection(x, g, w_in, layer, cos, sin, tm):
    n, s, _ = x.shape
    tok = lambda w: pl.BlockSpec((1, tm, w), lambda i, b: (b, i, 0))
    const = lambda shape: pl.BlockSpec(shape, lambda i, b: (0,) * len(shape))
    tab = pl.BlockSpec((tm, LANES), lambda i, b: (i, 0))
    sub = lambda r: pl.BlockSpec((3, 1, r, tm // r, D_B), lambda i, b: (0, b, 0, i, 0))
    return pl.pallas_call(
        functools.partial(_proj_kernel, tm=tm),
        grid=(s // tm, n),
        in_specs=[tok(D_MODEL), const((1, D_MODEL)), _layer_weight((D_MODEL, D_IN), layer),
                  tab, tab],
        out_specs=[tok(D_A), tok(D_KV_A), tok(D_KV_A),
                   pl.BlockSpec((3, 1, tm, D_B), lambda i, b: (0, b, i, 0)), sub(4), sub(16)],
        out_shape=[jax.ShapeDtypeStruct((n, s, D_A), BF16),
                   jax.ShapeDtypeStruct((n, s, D_KV_A), BF16),
                   jax.ShapeDtypeStruct((n, s, D_KV_A), BF16),
                   jax.ShapeDtypeStruct((3, n, s, D_B), BF16),
                   jax.ShapeDtypeStruct((3, n, 4, s // 4, D_B), BF16),
                   jax.ShapeDtypeStruct((3, n, 16, s // 16, D_B), BF16)],
        scratch_shapes=[pltpu.VMEM((N_SLABS, tm, LANES), F32),
                        pltpu.VMEM((N_SLABS, 4, tm // 4, LANES), F32)],
        compiler_params=_params(2),
        name="in_projection",
    )(x, g, w_in, cos, sin)


def _window_start(qi, tq, tk, seq, hw):
    return min(max(qi * tq - hw, 0), seq - tk)


def _band_tables(tq, tk, seq, hw, stack):
    offsets = sorted({qi * tq - _window_start(qi, tq, tk, seq, hw) for qi in range(seq // tq)})
    row = np.arange(stack * tq)[:, None] % tq
    col = np.arange(tk)[None, :]
    tabs = [np.where(np.abs(row + off - col) <= hw, 0.0, NEG) for off in offsets]
    return tuple(offsets), jnp.asarray(np.stack(tabs), F32)


def _band_window(qi, tq, tk, seq, hw, offsets, bias_ref):
    q0 = qi * tq
    start = pl.multiple_of(jnp.clip(q0 - hw, 0, seq - tk), hw)
    case = 0
    for idx, off in enumerate(offsets):
        case = jnp.where(q0 - start == off, idx, case)
    return start, bias_ref[case]


def _softmax_rows(s, sink):
    m = jnp.max(s, axis=-1, keepdims=True)
    if sink is not None:
        m = jnp.maximum(m, sink)
    p = jnp.exp2(s - m)
    den = jnp.sum(p, axis=-1, keepdims=True)
    if sink is not None:
        den = den + jnp.exp2(sink - m)
    return p.astype(BF16), m, den


def _qk(qs, k2):
    return lax.dot_general(qs, k2, (((1,), (1,)), ((), ())), preferred_element_type=F32)


def _attn_a_kernel(sink_ref, q_ref, k_ref, v_ref, g_ref, bias_ref, o_ref,
                   *, tq, nsub, tk, seq, hw, offsets):
    for t in range(nsub):
        start, bias = _band_window(pl.program_id(1) * nsub + t, tq, tk, seq, hw, offsets, bias_ref)
        _attn_a_tile(sink_ref, q_ref, k_ref, v_ref, g_ref, o_ref, start, bias,
                     slice(t * tq, (t + 1) * tq), tq, tk)


def _attn_a_tile(sink_ref, q_ref, k_ref, v_ref, g_ref, o_ref, start, bias, rows, tq, tk):
    group = N_HEADS_A // N_KV_A
    k2 = k_ref[0, pl.ds(start, tk), :]
    v2 = v_ref[0, pl.ds(start, tk), :]
    v2r = _roll_heads(v2)
    lo = _lane_lo((tq, LANES))
    zero = jnp.zeros((tq, LANES), BF16)
    block = lax.broadcasted_iota(jnp.int32, (group * tq, 1), 0) // tq
    tiles = [None] * (N_HEADS_A // 2)
    for kv in range(N_KV_A):
        pa, pb = 2 * kv, 2 * kv + 1
        qa = q_ref[0, rows, pa * LANES:(pa + 1) * LANES]
        qb = q_ref[0, rows, pb * LANES:(pb + 1) * LANES]
        qar, qbr = _roll_heads(qa), _roll_heads(qb)
        if kv == 0:
            parts = [jnp.where(lo, qa, zero), jnp.where(lo, qb, zero),
                     jnp.where(lo, qar, zero), jnp.where(lo, qbr, zero)]
            v_even, v_odd = v2, v2r
        else:
            parts = [jnp.where(lo, zero, qar), jnp.where(lo, zero, qbr),
                     jnp.where(lo, zero, qa), jnp.where(lo, zero, qb)]
            v_even, v_odd = v2r, v2
        heads = (2 * pa, 2 * pb, 2 * pa + 1, 2 * pb + 1)
        sink = jnp.zeros((group * tq, 1), F32)
        for blk, head in enumerate(heads):
            sink = jnp.where(block == blk, sink_ref[head] * LOG2E, sink)
        p, _, den = _softmax_rows(_qk(jnp.concatenate(parts, axis=0), k2) + bias, sink)
        oe = jnp.dot(p[0:2 * tq], v_even, preferred_element_type=F32)
        oo = jnp.dot(p[2 * tq:], v_odd, preferred_element_type=F32)
        tiles[pa] = (jnp.where(lo, oe[0:tq], oo[0:tq])
                     * (1.0 / jnp.where(lo, den[0:tq], den[2 * tq:3 * tq])))
        tiles[pb] = (jnp.where(lo, oe[tq:], oo[tq:])
                     * (1.0 / jnp.where(lo, den[tq:2 * tq], den[3 * tq:])))
    sumsq = sum(jnp.sum(t * t, axis=-1, keepdims=True) for t in tiles)
    inv = lax.rsqrt(sumsq * (1.0 / D_A) + EPS)
    for c, t in enumerate(tiles):
        o_ref[0, rows, c * LANES:(c + 1) * LANES] = (
            t * inv * g_ref[:, c * LANES:(c + 1) * LANES]).astype(BF16)


def _attention_a(q, k, v, sink, g_oa, tq, nsub):
    n, s, _ = q.shape
    hw = WINDOW_A
    tk = min(tq + 2 * hw, s)
    rows = tq * nsub
    offsets, bias = _band_tables(tq, tk, s, hw, N_HEADS_A // N_KV_A)
    kern = functools.partial(_attn_a_kernel, tq=tq, nsub=nsub, tk=tk, seq=s, hw=hw,
                             offsets=offsets)
    return pl.pallas_call(
        kern,
        grid=(n, s // rows),
        in_specs=[
            pl.BlockSpec(memory_space=pltpu.SMEM),
            pl.BlockSpec((1, rows, D_A), lambda b, i: (b, i, 0)),
            pl.BlockSpec((1, s, D_KV_A), lambda b, i: (b, 0, 0)),
            pl.BlockSpec((1, s, D_KV_A), lambda b, i: (b, 0, 0)),
            pl.BlockSpec((1, D_A), lambda b, i: (0, 0)),
            pl.BlockSpec(bias.shape, lambda b, i: (0, 0, 0)),
        ],
        out_specs=pl.BlockSpec((1, rows, D_A), lambda b, i: (b, i, 0)),
        out_shape=jax.ShapeDtypeStruct((n, s, D_A), BF16),
        compiler_params=_params(2),
        name="attention_a",
    )(sink, q, k, v, g_oa, bias)


B_UNROLL = 16
B_TQ = 128
B_HW = 64
assert all(w // (2 * r) == B_HW for w, r in DILATION_PAIRS)
assert tuple(r for _, r in DILATION_PAIRS) == (1, 4, 16)


def _dilated_kernel(q1_ref, k1_ref, v1_ref, q4_ref, k4_ref, v4_ref, q16_ref, k16_ref, v16_ref,
                    bias1_ref, bias4_ref, bias16_ref, o_ref, acc_ref, m_ref, l_ref,
                    *, tn, seq, tables):
    tq = B_TQ
    step = pl.program_id(2)
    lo = _lane_lo((tq, LANES))
    zero = jnp.zeros((tq, LANES), BF16)

    def unit(refs, r, c, t):
        q_ref, k_ref, v_ref, bias_ref = refs
        sub = seq // r
        tk, offsets = tables[r]
        qi = step * (tn // r // tq) + t
        start, bias = _band_window(qi, tq, tk, sub, B_HW, offsets, bias_ref)
        q2 = q_ref[0, 0, c, pl.ds(pl.multiple_of(t * tq, tq), tq), :]
        k2 = k_ref[0, 0, c, pl.ds(start, tk), :]
        v2 = v_ref[0, 0, c, pl.ds(start, tk), :]
        qs = jnp.concatenate([jnp.where(lo, q2, zero), jnp.where(lo, zero, q2)], axis=0)
        p, m, den = _softmax_rows(_qk(qs, k2) + bias, None)
        o = jnp.dot(p, v2, preferred_element_type=F32)
        return (jnp.where(lo, o[0:tq], o[tq:]), jnp.where(lo, m[0:tq], m[tq:]),
                jnp.where(lo, den[0:tq], den[tq:]))

    def merge(rows, o, m, l):
        m_old = m_ref[rows, :]
        m_new = jnp.maximum(m_old, m)
        a = jnp.exp2(m_old - m_new)
        b = jnp.exp2(m - m_new)
        return acc_ref[rows, :] * a + o * b, m_new, l_ref[rows, :] * a + l * b

    refs16 = (q16_ref, k16_ref, v16_ref, bias16_ref)
    refs4 = (q4_ref, k4_ref, v4_ref, bias4_ref)
    refs1 = (q1_ref, k1_ref, v1_ref, bias1_ref)

    def first(grp, carry):
        for cc in range(B_UNROLL):
            c = grp * B_UNROLL + cc
            for t in range(tn // 16 // tq):
                rows = pl.ds(16 * t * tq + c, tq, stride=16)
                acc_ref[rows, :], m_ref[rows, :], l_ref[rows, :] = unit(refs16, 16, c, t)
        return carry
    lax.fori_loop(0, 16 // B_UNROLL, first, 0)

    def second(grp, carry):
        for tt in range(B_UNROLL // 4):
            t = grp * (B_UNROLL // 4) + tt
            for c in range(4):
                rows = pl.ds(4 * t * tq + c, tq, stride=4)
                acc_ref[rows, :], m_ref[rows, :], l_ref[rows, :] = merge(
                    rows, *unit(refs4, 4, c, t))
        return carry
    lax.fori_loop(0, tn // tq // B_UNROLL, second, 0)

    def third(grp, carry):
        for tt in range(B_UNROLL):
            t = grp * B_UNROLL + tt
            rows = pl.ds(pl.multiple_of(t * tq, tq), tq)
            acc, _, l = merge(rows, *unit(refs1, 1, 0, t))
            o_ref[0, 0, rows, :] = acc * (1.0 / l)
        return carry
    lax.fori_loop(0, tn // tq // B_UNROLL, third, 0)


def _dilated_mixture(b1, b4, b16, tn):
    _, n, _, seq, _ = b1.shape
    assert tn % (16 * B_TQ) == 0 and seq % tn == 0
    arrays = {1: b1, 4: b4, 16: b16}
    tables, biases, specs, operands = {}, {}, [], []
    for r in (1, 4, 16):
        sub = seq // r
        tk = min(B_TQ + 2 * B_HW, sub)
        offsets, biases[r] = _band_tables(B_TQ, tk, sub, B_HW, 2)
        tables[r] = (tk, offsets)
        specs.append(pl.BlockSpec((1, 1, r, tn // r, LANES), lambda b, p, i: (0, b, 0, i, p)))
        specs.append(pl.BlockSpec((1, 1, r, sub, LANES), lambda b, p, i: (1, b, 0, 0, p)))
        specs.append(pl.BlockSpec((1, 1, r, sub, LANES), lambda b, p, i: (2, b, 0, 0, p)))
        operands += [arrays[r]] * 3
    for r in (1, 4, 16):
        specs.append(pl.BlockSpec(biases[r].shape, lambda b, p, i: (0, 0, 0)))
        operands.append(biases[r])
    kern = functools.partial(_dilated_kernel, tn=tn, seq=seq, tables=tables)
    return pl.pallas_call(
        kern,
        grid=(n, N_SLABS, seq // tn),
        in_specs=specs,
        out_specs=pl.BlockSpec((1, 1, tn, LANES), lambda b, p, i: (b, p, i, 0)),
        out_shape=jax.ShapeDtypeStruct((n, N_SLABS, seq, LANES), F32),
        scratch_shapes=[pltpu.VMEM((tn, LANES), F32)] * 3,
        compiler_params=_params(3),
        name="dilated_mixture",
    )(*operands)


def _mix_rows(rows, x_ref, oa_ref, ob_ref, g_ref, w_ref):
    tiles = [ob_ref[0, c, rows, :] for c in range(N_SLABS)]
    sumsq = sum(jnp.sum(t * t, axis=-1, keepdims=True) for t in tiles)
    rinv = lax.rsqrt(sumsq * (1.0 / D_B) + EPS)
    ob = jnp.concatenate(
        [(t * rinv * g_ref[:, c * LANES:(c + 1) * LANES]).astype(BF16)
         for c, t in enumerate(tiles)], axis=1)
    mix = jnp.concatenate([oa_ref[0, rows, :], ob], axis=1)
    return x_ref[0, rows, :] + jnp.dot(mix, w_ref[...], preferred_element_type=F32)


def _cross_rows(x, g_ref, wq_ref, kv_ref, wo_ref):
    h = _rms(x, g_ref[...]).astype(BF16)
    scale = HEAD_DIM_X ** -0.5 * LOG2E
    q = (jnp.dot(h, wq_ref[...], preferred_element_type=F32) * scale).astype(BF16)
    heads = []
    for hd in range(N_HEADS_X):
        cols = slice(hd * HEAD_DIM_X, (hd + 1) * HEAD_DIM_X)
        k = kv_ref[0, :, cols]
        v = kv_ref[0, :, D_X + hd * HEAD_DIM_X:D_X + (hd + 1) * HEAD_DIM_X]
        p, _, den = _softmax_rows(_qk(q[:, cols], k), None)
        heads.append((jnp.dot(p, v, preferred_element_type=F32) * (1.0 / den)).astype(BF16))
    return x + jnp.dot(jnp.concatenate(heads, axis=1), wo_ref[...], preferred_element_type=F32)


def _post_mixer_kernel(x_ref, oa_ref, ob_ref, mem_ref, gob_ref, wout_ref, gx_ref, wq_ref,
                       gm_ref, wkv_ref, wo_ref, out_ref, kv_ref, *, ts, nsub):
    @pl.when(pl.program_id(1) == 0)
    def _():
        hm = _rms(mem_ref[0], gm_ref[...]).astype(BF16)
        kv_ref[0] = jnp.dot(hm, wkv_ref[...], preferred_element_type=F32).astype(BF16)

    for t in range(nsub):
        rows = slice(t * ts, (t + 1) * ts)
        x1 = _mix_rows(rows, x_ref, oa_ref, ob_ref, gob_ref, wout_ref)
        out_ref[0, rows, :] = _cross_rows(x1, gx_ref, wq_ref, kv_ref, wo_ref)


def _post_mixer(x, oa, ob, mem, g_ob, w_out, g_cross, w_cq, g_mem, w_ckv, w_co, layer, ts, nsub):
    n, s, _ = x.shape
    m = mem.shape[1]
    tm = ts * nsub
    tok = lambda w: pl.BlockSpec((1, tm, w), lambda b, i: (b, i, 0))
    slab = pl.BlockSpec((1, N_SLABS, tm, LANES), lambda b, i: (b, 0, i, 0))
    const = lambda shape: pl.BlockSpec(shape, lambda b, i: (0,) * len(shape))
    return pl.pallas_call(
        functools.partial(_post_mixer_kernel, ts=ts, nsub=nsub),
        grid=(n, s // tm),
        in_specs=[tok(D_MODEL), tok(D_A), slab,
                  pl.BlockSpec((1, m, D_MODEL), lambda b, i: (b, 0, 0)),
                  const((1, D_B)), _layer_weight((D_MIX, D_MODEL), layer), const((1, D_MODEL)),
                  _layer_weight((D_MODEL, D_X), layer), const((1, D_MODEL)),
                  _layer_weight((D_MODEL, 2 * D_X), layer), _layer_weight((D_X, D_MODEL), layer)],
        out_specs=tok(D_MODEL),
        out_shape=jax.ShapeDtypeStruct((n, s, D_MODEL), F32),
        scratch_shapes=[pltpu.VMEM((1, m, 2 * D_X), BF16)],
        compiler_params=_params(2),
        name="post_mixer",
    )(x, oa, ob, mem, g_ob, w_out, g_cross, w_cq, g_mem, w_ckv, w_co)


def _ffn_kernel(x_ref, xp_ref, xn_ref, g_ref, wgu_ref, cw_ref, cb_ref, wd_ref, gf_ref,
                out_ref, h_ref, act_ref, *, tm, fc, final_norm):
    i = pl.program_id(1)
    g = g_ref[...]
    h_ref[0:HALO_ROWS, :] = _rms(xp_ref[0], g).astype(BF16)
    h_ref[HALO_ROWS:HALO_ROWS + tm, :] = _rms(x_ref[0], g).astype(BF16)
    h_ref[HALO_ROWS + tm:, :] = _rms(xn_ref[0], g).astype(BF16)
    row = lax.broadcasted_iota(jnp.int32, (tm, fc), 0)
    last = lax.broadcasted_iota(jnp.int32, (HALO_ROWS, fc), 0) == HALO_ROWS - 1
    first_tile = i == 0
    last_tile = i == pl.num_programs(1) - 1
    for j in range(D_FF // fc):
        cols = slice(j * fc, (j + 1) * fc)
        gate = jnp.dot(h_ref[...], wgu_ref[:, cols], preferred_element_type=F32)
        up = jnp.dot(h_ref[HALO_ROWS:HALO_ROWS + tm, :], wgu_ref[:, D_FF + j * fc:D_FF + (j + 1) * fc],
                     preferred_element_type=F32)
        cur = gate[HALO_ROWS:HALO_ROWS + tm]
        prev_row = jnp.sum(jnp.where(last, gate[0:HALO_ROWS], 0.0), axis=0, keepdims=True)
        prev_row = jnp.where(first_tile, 0.0, prev_row)
        next_row = jnp.where(last_tile, 0.0, gate[HALO_ROWS + tm:HALO_ROWS + tm + 1])
        before = jnp.where(row == 0, prev_row, pltpu.roll(cur, 1, axis=0))
        after = jnp.where(row == tm - 1, next_row, pltpu.roll(cur, tm - 1, axis=0))
        conv = (before * cw_ref[0:1, cols] + cur * cw_ref[1:2, cols]
                + after * cw_ref[2:3, cols] + cb_ref[:, cols])
        act = 0.5 * conv * (1.0 + lax.erf(conv * (2.0 ** -0.5))) * up
        act_ref[:, cols] = act.astype(BF16)
    y = x_ref[0] + jnp.dot(act_ref[...], wd_ref[...], preferred_element_type=F32)
    if final_norm:
        y = _rms(y, gf_ref[...])
    out_ref[0] = y


def _conv_ffn(x, g_ffn, w_gu, conv_w, conv_b, w_down, g_final, layer, final_norm, tm, fc):
    n, s, _ = x.shape
    hb = tm // HALO_ROWS
    last_hb = s // HALO_ROWS - 1
    tok = pl.BlockSpec((1, tm, D_MODEL), lambda b, i: (b, i, 0))
    prev = pl.BlockSpec((1, HALO_ROWS, D_MODEL), lambda b, i: (b, jnp.maximum(i * hb - 1, 0), 0))
    nxt = pl.BlockSpec((1, HALO_ROWS, D_MODEL),
                       lambda b, i: (b, jnp.minimum((i + 1) * hb, last_hb), 0))
    const = lambda shape: pl.BlockSpec(shape, lambda b, i: (0,) * len(shape))
    kern = functools.partial(_ffn_kernel, tm=tm, fc=fc, final_norm=final_norm)
    return pl.pallas_call(
        kern,
        grid=(n, s // tm),
        in_specs=[tok, prev, nxt, const((1, D_MODEL)), _layer_weight((D_MODEL, 2 * D_FF), layer),
                  const((3, D_FF)), const((1, D_FF)), _layer_weight((D_FF, D_MODEL), layer),
                  const((1, D_MODEL))],
        out_specs=tok,
        out_shape=jax.ShapeDtypeStruct((n, s, D_MODEL), F32),
        scratch_shapes=[pltpu.VMEM((tm + 2 * HALO_ROWS, D_MODEL), BF16),
                        pltpu.VMEM((tm, D_FF), BF16)],
        compiler_params=_params(2),
        name="conv_ffn",
    )(x, x, x, g_ffn, w_gu, conv_w, conv_b, w_down, g_final)


def _rope_tables(seq):
    inv = 1.0 / (ROPE_THETA ** (jnp.arange(0, HEAD_DIM, 2, dtype=F32) / HEAD_DIM))
    ang = jnp.arange(seq, dtype=F32)[:, None] * inv[None, :]
    cos, sin = jnp.cos(ang), jnp.sin(ang)
    reps = LANES // HEAD_DIM
    return (jnp.tile(jnp.concatenate([cos, cos], axis=-1), (1, reps)),
            jnp.tile(jnp.concatenate([-sin, sin], axis=-1), (1, reps)))


def _trunk(x, mem, p):
    n, s, _ = x.shape
    cos, sin = _rope_tables(s)
    row = lambda v: v.reshape(1, -1)
    for l in range(DEPTH):
        qa, ka, va, b1, b4, b16 = _in_projection(
            x, row(p['g_mix'][l]), p['w_in'], l, cos, sin, tm=1024)
        oa = _attention_a(qa, ka, va, p['sink_a'][l], row(p['g_out_a'][l]), tq=128, nsub=8)
        ob = _dilated_mixture(b1.reshape(3, n, 1, s, D_B), b4, b16, tn=2048)
        x = _post_mixer(x, oa, ob, mem, row(p['g_out_b'][l]), p['w_out'], row(p['g_cross'][l]),
                        p['w_cq'], row(p['g_mem'][l]), p['w_ckv'], p['w_co'], l, ts=512, nsub=2)
        x = _conv_ffn(x, row(p['g_ffn'][l]), p['w_gu'], p['conv_w'][l],
                      row(p['conv_b'][l]), p['w_down'], row(p['g_final']), l,
                      final_norm=(l == DEPTH - 1), tm=1024, fc=256)
    return x


def kernel(x_prompt, x_sample, mem_prompt, mem_sample, g_mix, w_in, sink_a, g_out_a, g_out_b, w_out, g_cross, g_mem, w_cq, w_ckv, w_co, g_ffn, w_gu, conv_w, conv_b, w_down, g_final):
    p = dict(g_mix=g_mix, sink_a=sink_a, g_out_a=g_out_a, g_out_b=g_out_b, g_cross=g_cross,
             g_mem=g_mem, g_ffn=g_ffn, conv_w=conv_w, conv_b=conv_b, g_final=g_final)
    weights = dict(w_in=w_in, w_out=w_out, w_cq=w_cq, w_ckv=w_ckv, w_co=w_co, w_gu=w_gu,
                   w_down=w_down)
    for name, w in weights.items():
        p[name] = w.astype(BF16)
    return (_trunk(x_prompt, mem_prompt, p), _trunk(x_sample, mem_sample, p))
```

```python
import functools
import math

import numpy as np
import jax
import jax.numpy as jnp
from jax import lax
from jax.experimental import pallas as pl
from jax.experimental.pallas import tpu as pltpu

D_MODEL = 1024
DEPTH = 2
HEAD_DIM = 64
N_HEADS_A = 8
N_KV_A = 2
N_HEADS_B = 8
D_A = N_HEADS_A * HEAD_DIM
D_KV_A = N_KV_A * HEAD_DIM
D_B = N_HEADS_B * HEAD_DIM
D_MIX = D_A + D_B
D_IN = D_A + 2 * D_KV_A + 3 * D_B
WINDOW_A = 128
DILATION_PAIRS = ((128, 1), (512, 4), (2048, 16))
N_HEADS_X = 4
HEAD_DIM_X = 128
D_X = N_HEADS_X * HEAD_DIM_X
D_FF = 2816
ROPE_THETA = 10000.0
EPS = 1e-6
NEG = -1e30
LOG2E = math.log2(math.e)

LANES = 128
HALO_ROWS = 16
N_SLABS = D_B // LANES
VMEM_LIMIT = 56 * 1024 * 1024

F32 = jnp.float32
BF16 = jnp.bfloat16


def _params(n_axes):
    return pltpu.CompilerParams(
        dimension_semantics=("arbitrary",) * n_axes, vmem_limit_bytes=VMEM_LIMIT)


def _layer_weight(shape, layer):
    return pl.BlockSpec((None,) + tuple(shape), lambda *_: (layer,) + (0,) * len(shape),
                        pipeline_mode=pl.Buffered(1))


def _rms(xf, g):
    return xf * lax.rsqrt(jnp.mean(xf * xf, axis=-1, keepdims=True) + EPS) * g


def _lane_lo(shape):
    return lax.broadcasted_iota(jnp.int32, shape, 1) % LANES < HEAD_DIM


def _roll_heads(t):
    return pltpu.roll(t.astype(F32), HEAD_DIM, axis=1).astype(t.dtype)


def _proj_kernel(x_ref, g_ref, w_ref, cos_ref, sin_ref,
                 qa_ref, ka_ref, va_ref, b1_ref, b4_ref, b16_ref, nat_ref, sub_ref, *, tm):
    h = _rms(x_ref[0], g_ref[...]).astype(BF16)
    cos = cos_ref[...]
    sin = sin_ref[...]
    first_half = lax.broadcasted_iota(jnp.int32, cos.shape, 1) % HEAD_DIM < HEAD_DIM // 2

    def rope(y, scale):
        partner = jnp.where(first_half,
                            pltpu.roll(y, LANES - HEAD_DIM // 2, axis=1),
                            pltpu.roll(y, HEAD_DIM // 2, axis=1))
        out = y * cos + partner * sin
        return out * scale if scale != 1.0 else out

    def project(col, width):
        return jnp.dot(h, w_ref[:, col:col + width], preferred_element_type=F32)

    q_scale = HEAD_DIM ** -0.5 * LOG2E
    ya = project(0, D_A)
    for c in range(D_A // LANES):
        cols = slice(c * LANES, (c + 1) * LANES)
        qa_ref[0, :, cols] = rope(ya[:, cols], q_scale).astype(BF16)
    ykv = project(D_A, 2 * D_KV_A)
    ka_ref[0] = rope(ykv[:, 0:D_KV_A], 1.0).astype(BF16)
    va_ref[0] = ykv[:, D_KV_A:].astype(BF16)

    off = D_A + 2 * D_KV_A
    for part, (roped, scale) in enumerate(((True, q_scale), (True, 1.0), (False, 1.0))):
        yb = project(off + part * D_B, D_B)
        for c in range(N_SLABS):
            cols = slice(c * LANES, (c + 1) * LANES)
            y = rope(yb[:, cols], scale) if roped else yb[:, cols]
            b1_ref[part, 0, :, cols] = y.astype(BF16)
            nat_ref[c] = y
            for c0 in range(4):
                t4 = nat_ref[c, pl.ds(c0, tm // 4, stride=4), :]
                b4_ref[part, 0, c0, :, cols] = t4.astype(BF16)
                sub_ref[c, c0] = t4
            for c0 in range(4):
                for c1 in range(4):
                    t16 = sub_ref[c, c0, pl.ds(c1, tm // 16, stride=4), :]
                    b16_ref[part, 0, c0 + 4 * c1, :, cols] = t16.astype(BF16)


def _in_projection(x, g, w_in, layer, cos, sin, tm):
    n, s, _ = x.shape
    tok = lambda w: pl.BlockSpec((1, tm, w), lambda i, b: (b, i, 0))
    const = lambda shape: pl.BlockSpec(shape, lambda i, b: (0,) * len(shape))
    tab = pl.BlockSpec((tm, LANES), lambda i, b: (i, 0))
    sub = lambda r: pl.BlockSpec((3, 1, r, tm // r, D_B), lambda i, b: (0, b, 0, i, 0))
    return pl.pallas_call(
        functools.partial(_proj_kernel, tm=tm),
        grid=(s // tm, n),
        in_specs=[tok(D_MODEL), const((1, D_MODEL)), _layer_weight((D_MODEL, D_IN), layer),
                  tab, tab],
        out_specs=[tok(D_A), tok(D_KV_A), tok(D_KV_A),
                   pl.BlockSpec((3, 1, tm, D_B), lambda i, b: (0, b, i, 0)), sub(4), sub(16)],
        out_shape=[jax.ShapeDtypeStruct((n, s, D_A), BF16),
                   jax.ShapeDtypeStruct((n, s, D_KV_A), BF16),
                   jax.ShapeDtypeStruct((n, s, D_KV_A), BF16),
                   jax.ShapeDtypeStruct((3, n, s, D_B), BF16),
                   jax.ShapeDtypeStruct((3, n, 4, s // 4, D_B), BF16),
                   jax.ShapeDtypeStruct((3, n, 16, s // 16, D_B), BF16)],
        scratch_shapes=[pltpu.VMEM((N_SLABS, tm, LANES), F32),
                        pltpu.VMEM((N_SLABS, 4, tm // 4, LANES), F32)],
        compiler_params=_params(2),
        name="in_projection",
    )(x, g, w_in, cos, sin)


def _window_start(qi, tq, tk, seq, hw):
    return min(max(qi * tq - hw, 0), seq - tk)


def _band_tables(tq, tk, seq, hw, stack):
    offsets = sorted({qi * tq - _window_start(qi, tq, tk, seq, hw) for qi in range(seq // tq)})
    row = np.arange(stack * tq)[:, None] % tq
    col = np.arange(tk)[None, :]
    tabs = [np.where(np.abs(row + off - col) <= hw, 0.0, NEG) for off in offsets]
    return tuple(offsets), jnp.asarray(np.stack(tabs), F32)


def _band_window(qi, tq, tk, seq, hw, offsets, bias_ref):
    q0 = qi * tq
    start = pl.multiple_of(jnp.clip(q0 - hw, 0, seq - tk), hw)
    case = 0
    for idx, off in enumerate(offsets):
        case = jnp.where(q0 - start == off, idx, case)
    return start, bias_ref[case]


def _softmax_rows(s, sink):
    m = jnp.max(s, axis=-1, keepdims=True)
    if sink is not None:
        m = jnp.maximum(m, sink)
    p = jnp.exp2(s - m)
    den = jnp.sum(p, axis=-1, keepdims=True)
    if sink is not None:
        den = den + jnp.exp2(sink - m)
    return p.astype(BF16), m, den


def _qk(qs, k2):
    return lax.dot_general(qs, k2, (((1,), (1,)), ((), ())), preferred_element_type=F32)


def _attn_a_kernel(sink_ref, q_ref, k_ref, v_ref, g_ref, bias_ref, o_ref,
                   *, tq, nsub, tk, seq, hw, offsets):
    for t in range(nsub):
        start, bias = _band_window(pl.program_id(1) * nsub + t, tq, tk, seq, hw, offsets, bias_ref)
        _attn_a_tile(sink_ref, q_ref, k_ref, v_ref, g_ref, o_ref, start, bias,
                     slice(t * tq, (t + 1) * tq), tq, tk)


def _attn_a_tile(sink_ref, q_ref, k_ref, v_ref, g_ref, o_ref, start, bias, rows, tq, tk):
    group = N_HEADS_A // N_KV_A
    k2 = k_ref[0, pl.ds(start, tk), :]
    v2 = v_ref[0, pl.ds(start, tk), :]
    v2r = _roll_heads(v2)
    lo = _lane_lo((tq, LANES))
    zero = jnp.zeros((tq, LANES), BF16)
    block = lax.broadcasted_iota(jnp.int32, (group * tq, 1), 0) // tq
    tiles = [None] * (N_HEADS_A // 2)
    for kv in range(N_KV_A):
        pa, pb = 2 * kv, 2 * kv + 1
        qa = q_ref[0, rows, pa * LANES:(pa + 1) * LANES]
        qb = q_ref[0, rows, pb * LANES:(pb + 1) * LANES]
        qar, qbr = _roll_heads(qa), _roll_heads(qb)
        if kv == 0:
            parts = [jnp.where(lo, qa, zero), jnp.where(lo, qb, zero),
                     jnp.where(lo, qar, zero), jnp.where(lo, qbr, zero)]
            v_even, v_odd = v2, v2r
        else:
            parts = [jnp.where(lo, zero, qar), jnp.where(lo, zero, qbr),
                     jnp.where(lo, zero, qa), jnp.where(lo, zero, qb)]
            v_even, v_odd = v2r, v2
        heads = (2 * pa, 2 * pb, 2 * pa + 1, 2 * pb + 1)
        sink = jnp.zeros((group * tq, 1), F32)
        for blk, head in enumerate(heads):
            sink = jnp.where(block == blk, sink_ref[head] * LOG2E, sink)
        p, _, den = _softmax_rows(_qk(jnp.concatenate(parts, axis=0), k2) + bias, sink)
        oe = jnp.dot(p[0:2 * tq], v_even, preferred_element_type=F32)
        oo = jnp.dot(p[2 * tq:], v_odd, preferred_element_type=F32)
        tiles[pa] = (jnp.where(lo, oe[0:tq], oo[0:tq])
                     * (1.0 / jnp.where(lo, den[0:tq], den[2 * tq:3 * tq])))
        tiles[pb] = (jnp.where(lo, oe[tq:], oo[tq:])
                     * (1.0 / jnp.where(lo, den[tq:2 * tq], den[3 * tq:])))
    sumsq = sum(jnp.sum(t * t, axis=-1, keepdims=True) for t in tiles)
    inv = lax.rsqrt(sumsq * (1.0 / D_A) + EPS)
    for c, t in enumerate(tiles):
        o_ref[0, rows, c * LANES:(c + 1) * LANES] = (
            t * inv * g_ref[:, c * LANES:(c + 1) * LANES]).astype(BF16)


def _attention_a(q, k, v, sink, g_oa, tq, nsub):
    n, s, _ = q.shape
    hw = WINDOW_A
    tk = min(tq + 2 * hw, s)
    rows = tq * nsub
    offsets, bias = _band_tables(tq, tk, s, hw, N_HEADS_A // N_KV_A)
    kern = functools.partial(_attn_a_kernel, tq=tq, nsub=nsub, tk=tk, seq=s, hw=hw,
                             offsets=offsets)
    return pl.pallas_call(
        kern,
        grid=(n, s // rows),
        in_specs=[
            pl.BlockSpec(memory_space=pltpu.SMEM),
            pl.BlockSpec((1, rows, D_A), lambda b, i: (b, i, 0)),
            pl.BlockSpec((1, s, D_KV_A), lambda b, i: (b, 0, 0)),
            pl.BlockSpec((1, s, D_KV_A), lambda b, i: (b, 0, 0)),
            pl.BlockSpec((1, D_A), lambda b, i: (0, 0)),
            pl.BlockSpec(bias.shape, lambda b, i: (0, 0, 0)),
        ],
        out_specs=pl.BlockSpec((1, rows, D_A), lambda b, i: (b, i, 0)),
        out_shape=jax.ShapeDtypeStruct((n, s, D_A), BF16),
        compiler_params=_params(2),
        name="attention_a",
    )(sink, q, k, v, g_oa, bias)


B_UNROLL = 16
B_TQ = 128
B_HW = 64
assert all(w // (2 * r) == B_HW for w, r in DILATION_PAIRS)
assert tuple(r for _, r in DILATION_PAIRS) == (1, 4, 16)


def _dilated_kernel(qkv1_ref, qkv4_ref, qkv16_ref, bias1_ref, bias4_ref, bias16_ref,
                    o_ref, acc_ref, m_ref, l_ref, *, tn, seq, tables):
    tq = B_TQ
    step = pl.program_id(2)
    lo = _lane_lo((tq, LANES))
    zero = jnp.zeros((tq, LANES), BF16)

    def unit(refs, r, c, t):
        qkv_ref, bias_ref = refs
        sub = seq // r
        tk, offsets = tables[r]
        qi = step * (tn // r // tq) + t
        start, bias = _band_window(qi, tq, tk, sub, B_HW, offsets, bias_ref)
        q2 = qkv_ref[0, 0, c, pl.ds(pl.multiple_of(qi * tq, tq), tq), :]
        k2 = qkv_ref[1, 0, c, pl.ds(start, tk), :]
        v2 = qkv_ref[2, 0, c, pl.ds(start, tk), :]
        qs = jnp.concatenate([jnp.where(lo, q2, zero), jnp.where(lo, zero, q2)], axis=0)
        p, m, den = _softmax_rows(_qk(qs, k2) + bias, None)
        o = jnp.dot(p, v2, preferred_element_type=F32)
        return (jnp.where(lo, o[0:tq], o[tq:]), jnp.where(lo, m[0:tq], m[tq:]),
                jnp.where(lo, den[0:tq], den[tq:]))

    def merge(rows, o, m, l):
        m_old = m_ref[rows, :]
        m_new = jnp.maximum(m_old, m)
        a = jnp.exp2(m_old - m_new)
        b = jnp.exp2(m - m_new)
        return acc_ref[rows, :] * a + o * b, m_new, l_ref[rows, :] * a + l * b

    refs16 = (qkv16_ref, bias16_ref)
    refs4 = (qkv4_ref, bias4_ref)
    refs1 = (qkv1_ref, bias1_ref)

    def first(grp, carry):
        for cc in range(B_UNROLL):
            c = grp * B_UNROLL + cc
            for t in range(tn // 16 // tq):
                rows = pl.ds(16 * t * tq + c, tq, stride=16)
                acc_ref[rows, :], m_ref[rows, :], l_ref[rows, :] = unit(refs16, 16, c, t)
        return carry
    lax.fori_loop(0, 16 // B_UNROLL, first, 0)

    def second(grp, carry):
        for tt in range(B_UNROLL // 4):
            t = grp * (B_UNROLL // 4) + tt
            for c in range(4):
                rows = pl.ds(4 * t * tq + c, tq, stride=4)
                acc_ref[rows, :], m_ref[rows, :], l_ref[rows, :] = merge(
                    rows, *unit(refs4, 4, c, t))
        return carry
    lax.fori_loop(0, tn // tq // B_UNROLL, second, 0)

    def third(grp, carry):
        for tt in range(B_UNROLL):
            t = grp * B_UNROLL + tt
            rows = pl.ds(pl.multiple_of(t * tq, tq), tq)
            acc, _, l = merge(rows, *unit(refs1, 1, 0, t))
            o_ref[0, 0, rows, :] = acc * (1.0 / l)
        return carry
    lax.fori_loop(0, tn // tq // B_UNROLL, third, 0)


def _dilated_mixture(b1, b4, b16, tn):
    _, n, _, seq, _ = b1.shape
    assert tn % (16 * B_TQ) == 0 and seq % tn == 0
    arrays = {1: b1, 4: b4, 16: b16}
    tables, biases, specs, operands = {}, {}, [], []
    for r in (1, 4, 16):
        sub = seq // r
        tk = min(B_TQ + 2 * B_HW, sub)
        offsets, biases[r] = _band_tables(B_TQ, tk, sub, B_HW, 2)
        tables[r] = (tk, offsets)
        specs.append(pl.BlockSpec((3, 1, r, sub, LANES), lambda b, p, i: (0, b, 0, 0, p)))
        operands.append(arrays[r])
    for r in (1, 4, 16):
        specs.append(pl.BlockSpec(biases[r].shape, lambda b, p, i: (0, 0, 0)))
        operands.append(biases[r])
    kern = functools.partial(_dilated_kernel, tn=tn, seq=seq, tables=tables)
    return pl.pallas_call(
        kern,
        grid=(n, N_SLABS, seq // tn),
        in_specs=specs,
        out_specs=pl.BlockSpec((1, 1, tn, LANES), lambda b, p, i: (b, p, i, 0)),
        out_shape=jax.ShapeDtypeStruct((n, N_SLABS, seq, LANES), F32),
        scratch_shapes=[pltpu.VMEM((tn, LANES), F32)] * 3,
        compiler_params=_params(3),
        name="dilated_mixture",
    )(*operands)


def _mix_rows(rows, x_ref, oa_ref, ob_ref, g_ref, w_ref):
    tiles = [ob_ref[0, c, rows, :] for c in range(N_SLABS)]
    sumsq = sum(jnp.sum(t * t, axis=-1, keepdims=True) for t in tiles)
    rinv = lax.rsqrt(sumsq * (1.0 / D_B) + EPS)
    ob = jnp.concatenate(
        [(t * rinv * g_ref[:, c * LANES:(c + 1) * LANES]).astype(BF16)
         for c, t in enumerate(tiles)], axis=1)
    mix = jnp.concatenate([oa_ref[0, rows, :], ob], axis=1)
    return x_ref[0, rows, :] + jnp.dot(mix, w_ref[...], preferred_element_type=F32)


def _cross_rows(x, g_ref, wq_ref, kv_ref, wo_ref):
    h = _rms(x, g_ref[...]).astype(BF16)
    scale = HEAD_DIM_X ** -0.5 * LOG2E
    q = (jnp.dot(h, wq_ref[...], preferred_element_type=F32) * scale).astype(BF16)
    heads = []
    for hd in range(N_HEADS_X):
        cols = slice(hd * HEAD_DIM_X, (hd + 1) * HEAD_DIM_X)
        k = kv_ref[0, :, cols]
        v = kv_ref[0, :, D_X + hd * HEAD_DIM_X:D_X + (hd + 1) * HEAD_DIM_X]
        p, _, den = _softmax_rows(_qk(q[:, cols], k), None)
        heads.append((jnp.dot(p, v, preferred_element_type=F32) * (1.0 / den)).astype(BF16))
    return x + jnp.dot(jnp.concatenate(heads, axis=1), wo_ref[...], preferred_element_type=F32)


def _post_mixer_kernel(x_ref, oa_ref, ob_ref, mem_ref, gob_ref, wout_ref, gx_ref, wq_ref,
                       gm_ref, wkv_ref, wo_ref, out_ref, kv_ref, *, ts, nsub):
    @pl.when(pl.program_id(1) == 0)
    def _():
        hm = _rms(mem_ref[0], gm_ref[...]).astype(BF16)
        kv_ref[0] = jnp.dot(hm, wkv_ref[...], preferred_element_type=F32).astype(BF16)

    for t in range(nsub):
        rows = slice(t * ts, (t + 1) * ts)
        x1 = _mix_rows(rows, x_ref, oa_ref, ob_ref, gob_ref, wout_ref)
        out_ref[0, rows, :] = _cross_rows(x1, gx_ref, wq_ref, kv_ref, wo_ref)


def _post_mixer(x, oa, ob, mem, g_ob, w_out, g_cross, w_cq, g_mem, w_ckv, w_co, layer, ts, nsub):
    n, s, _ = x.shape
    m = mem.shape[1]
    tm = ts * nsub
    tok = lambda w: pl.BlockSpec((1, tm, w), lambda b, i: (b, i, 0))
    slab = pl.BlockSpec((1, N_SLABS, tm, LANES), lambda b, i: (b, 0, i, 0))
    const = lambda shape: pl.BlockSpec(shape, lambda b, i: (0,) * len(shape))
    return pl.pallas_call(
        functools.partial(_post_mixer_kernel, ts=ts, nsub=nsub),
        grid=(n, s // tm),
        in_specs=[tok(D_MODEL), tok(D_A), slab,
                  pl.BlockSpec((1, m, D_MODEL), lambda b, i: (b, 0, 0)),
                  const((1, D_B)), _layer_weight((D_MIX, D_MODEL), layer), const((1, D_MODEL)),
                  _layer_weight((D_MODEL, D_X), layer), const((1, D_MODEL)),
                  _layer_weight((D_MODEL, 2 * D_X), layer), _layer_weight((D_X, D_MODEL), layer)],
        out_specs=tok(D_MODEL),
        out_shape=jax.ShapeDtypeStruct((n, s, D_MODEL), F32),
        scratch_shapes=[pltpu.VMEM((1, m, 2 * D_X), BF16)],
        compiler_params=_params(2),
        name="post_mixer",
    )(x, oa, ob, mem, g_ob, w_out, g_cross, w_cq, g_mem, w_ckv, w_co)


def _ffn_kernel(x_ref, xp_ref, xn_ref, g_ref, wgu_ref, cw_ref, cb_ref, wd_ref, gf_ref,
                out_ref, h_ref, act_ref, *, tm, fc, final_norm):
    i = pl.program_id(1)
    g = g_ref[...]
    h_ref[0:HALO_ROWS, :] = _rms(xp_ref[0], g).astype(BF16)
    h_ref[HALO_ROWS:HALO_ROWS + tm, :] = _rms(x_ref[0], g).astype(BF16)
    h_ref[HALO_ROWS + tm:, :] = _rms(xn_ref[0], g).astype(BF16)
    row = lax.broadcasted_iota(jnp.int32, (tm, fc), 0)
    last = lax.broadcasted_iota(jnp.int32, (HALO_ROWS, fc), 0) == HALO_ROWS - 1
    first_tile = i == 0
    last_tile = i == pl.num_programs(1) - 1
    for j in range(D_FF // fc):
        cols = slice(j * fc, (j + 1) * fc)
        gate = jnp.dot(h_ref[...], wgu_ref[:, cols], preferred_element_type=F32)
        up = jnp.dot(h_ref[HALO_ROWS:HALO_ROWS + tm, :], wgu_ref[:, D_FF + j * fc:D_FF + (j + 1) * fc],
                     preferred_element_type=F32)
        cur = gate[HALO_ROWS:HALO_ROWS + tm]
        prev_row = jnp.sum(jnp.where(last, gate[0:HALO_ROWS], 0.0), axis=0, keepdims=True)
        prev_row = jnp.where(first_tile, 0.0, prev_row)
        next_row = jnp.where(last_tile, 0.0, gate[HALO_ROWS + tm:HALO_ROWS + tm + 1])
        before = jnp.where(row == 0, prev_row, pltpu.roll(cur, 1, axis=0))
        after = jnp.where(row == tm - 1, next_row, pltpu.roll(cur, tm - 1, axis=0))
        conv = (before * cw_ref[0:1, cols] + cur * cw_ref[1:2, cols]
                + after * cw_ref[2:3, cols] + cb_ref[:, cols])
        act = 0.5 * conv * (1.0 + lax.erf(conv * (2.0 ** -0.5))) * up
        act_ref[:, cols] = act.astype(BF16)
    y = x_ref[0] + jnp.dot(act_ref[...], wd_ref[...], preferred_element_type=F32)
    if final_norm:
        y = _rms(y, gf_ref[...])
    out_ref[0] = y


def _conv_ffn(x, g_ffn, w_gu, conv_w, conv_b, w_down, g_final, layer, final_norm, tm, fc):
    n, s, _ = x.shape
    hb = tm // HALO_ROWS
    last_hb = s // HALO_ROWS - 1
    tok = pl.BlockSpec((1, tm, D_MODEL), lambda b, i: (b, i, 0))
    prev = pl.BlockSpec((1, HALO_ROWS, D_MODEL), lambda b, i: (b, jnp.maximum(i * hb - 1, 0), 0))
    nxt = pl.BlockSpec((1, HALO_ROWS, D_MODEL),
                       lambda b, i: (b, jnp.minimum((i + 1) * hb, last_hb), 0))
    const = lambda shape: pl.BlockSpec(shape, lambda b, i: (0,) * len(shape))
    kern = functools.partial(_ffn_kernel, tm=tm, fc=fc, final_norm=final_norm)
    return pl.pallas_call(
        kern,
        grid=(n, s // tm),
        in_specs=[tok, prev, nxt, const((1, D_MODEL)), _layer_weight((D_MODEL, 2 * D_FF), layer),
                  const((3, D_FF)), const((1, D_FF)), _layer_weight((D_FF, D_MODEL), layer),
                  const((1, D_MODEL))],
        out_specs=tok,
        out_shape=jax.ShapeDtypeStruct((n, s, D_MODEL), F32),
        scratch_shapes=[pltpu.VMEM((tm + 2 * HALO_ROWS, D_MODEL), BF16),
                        pltpu.VMEM((tm, D_FF), BF16)],
        compiler_params=_params(2),
        name="conv_ffn",
    )(x, x, x, g_ffn, w_gu, conv_w, conv_b, w_down, g_final)


def _rope_tables(seq):
    inv = 1.0 / (ROPE_THETA ** (jnp.arange(0, HEAD_DIM, 2, dtype=F32) / HEAD_DIM))
    ang = jnp.arange(seq, dtype=F32)[:, None] * inv[None, :]
    cos, sin = jnp.cos(ang), jnp.sin(ang)
    reps = LANES // HEAD_DIM
    return (jnp.tile(jnp.concatenate([cos, cos], axis=-1), (1, reps)),
            jnp.tile(jnp.concatenate([-sin, sin], axis=-1), (1, reps)))


def _trunk(x, mem, p):
    n, s, _ = x.shape
    cos, sin = _rope_tables(s)
    row = lambda v: v.reshape(1, -1)
    for l in range(DEPTH):
        qa, ka, va, b1, b4, b16 = _in_projection(
            x, row(p['g_mix'][l]), p['w_in'], l, cos, sin, tm=1024)
        oa = _attention_a(qa, ka, va, p['sink_a'][l], row(p['g_out_a'][l]), tq=128, nsub=8)
        ob = _dilated_mixture(b1.reshape(3, n, 1, s, D_B), b4, b16, tn=2048)
        x = _post_mixer(x, oa, ob, mem, row(p['g_out_b'][l]), p['w_out'], row(p['g_cross'][l]),
                        p['w_cq'], row(p['g_mem'][l]), p['w_ckv'], p['w_co'], l, ts=512, nsub=2)
        x = _conv_ffn(x, row(p['g_ffn'][l]), p['w_gu'], p['conv_w'][l],
                      row(p['conv_b'][l]), p['w_down'], row(p['g_final']), l,
                      final_norm=(l == DEPTH - 1), tm=1024, fc=256)
    return x


def kernel(x_prompt, x_sample, mem_prompt, mem_sample, g_mix, w_in, sink_a, g_out_a, g_out_b, w_out, g_cross, g_mem, w_cq, w_ckv, w_co, g_ffn, w_gu, conv_w, conv_b, w_down, g_final):
    p = dict(g_mix=g_mix, sink_a=sink_a, g_out_a=g_out_a, g_out_b=g_out_b, g_cross=g_cross,
             g_mem=g_mem, g_ffn=g_ffn, conv_w=conv_w, conv_b=conv_b, g_final=g_final)
    weights = dict(w_in=w_in, w_out=w_out, w_cq=w_cq, w_ckv=w_ckv, w_co=w_co, w_gu=w_gu,
                   w_down=w_down)
    for name, w in weights.items():
        p[name] = w.astype(BF16)
    return (_trunk(x_prompt, mem_prompt, p), _trunk(x_sample, mem_sample, p))
```
